```python
import jax
import jax.numpy as jnp
from jax import lax
import numpy as np

D_MODEL = 2048
BATCH = 4
SEQ = 2048
DEPTH = 4
DEC_BATCH = 128
DEC_SEQ = 4
PAST_LEN = 16384
PAGE_SIZE = 128

N_MIXERS = 3
N_A = (DEPTH + 2) // 3
N_B = (DEPTH + 1) // 3
N_C = DEPTH // 3
RMS_EPS = 1e-5

HEAD_SIZE = 64
N_HEADS = D_MODEL // HEAD_SIZE
D_DECAY_LORA = max(32, int(round(1.8 * D_MODEL ** 0.5 / 32)) * 32)
D_AAA_LORA = max(32, int(round(1.8 * D_MODEL ** 0.5 / 32)) * 32)
D_MV_LORA = max(32, int(round(1.3 * D_MODEL ** 0.5 / 32)) * 32)
D_GATE_LORA = max(32, int(round(0.6 * D_MODEL ** 0.8 / 32)) * 32)
LN_X_EPS = 64e-5

CONV_WIDTH = 31
D_CONV = D_MODEL
CONV_LN_EPS = 1e-5

POOL_WINDOWS = (2, 4, 8, 16)
POOL_GROUP = D_MODEL // len(POOL_WINDOWS)
POOL_BUF = max(POOL_WINDOWS) - 1

N_EXPERTS = 32
TOP_K = 4
D_EXPERT = D_MODEL
SWIGLU_LIMIT = 7.0
SWIGLU_ALPHA = 1.702
MOE_BLOCK = 128

kernel_name = 'hybrid_rwkv7_conv_pool_moe_adaln_step'


def _rmsnorm(x, g):
    xf = x.astype(jnp.float32)
    y = xf * lax.rsqrt(jnp.mean(xf * xf, axis=-1, keepdims=True) + RMS_EPS)
    return (y * g.astype(jnp.float32)).astype(x.dtype)


def _adaln(c, w, b):
    m = jax.nn.silu(c) @ w + b
    return jnp.split(m, 6, axis=-1)


def _modulate(h, shift, scale):
    return h * (1 + scale[:, None, :]) + shift[:, None, :]


def _wkv_step(S, inp):
    r, w, k, v, a, b = inp
    sa = jnp.einsum('bhvk,bhk->bhv', S, a)
    S = S * w[:, :, None, :] + sa[..., None] * b[:, :, None, :] + v[..., None] * k[:, :, None, :]
    return S, jnp.einsum('bhvk,bhk->bhv', S, r)


def _rwkv7(h, shift_prev, S0, v_first, vres, mix, wr, wk, wv, wo, w0, w1, w2,
           a0, a1, a2, g1, g2, k_k, k_a, r_k, lnx_w, lnx_b):
    B, T, D = h.shape
    f32 = jnp.float32
    xx = jnp.concatenate([shift_prev[:, None].astype(h.dtype), h[:, :-1]], axis=1) - h
    xr, xw, xk, xv, xa, xg = (h + xx * mix[m] for m in range(6))
    r = xr @ wr
    k = xk @ wk
    v = xv @ wv
    if vres is None:
        v_first = v
    else:
        v0, v1, v2 = vres
        v = v + (v_first - v) * jax.nn.sigmoid(v0 + (xv @ v1) @ v2)
    logw = -jax.nn.softplus(-(w0 + jnp.tanh(xw @ w1) @ w2).astype(f32)) - 0.5
    decay = jnp.exp(-jnp.exp(logw))
    a = jax.nn.sigmoid((a0 + (xa @ a1) @ a2).astype(f32))
    g = jax.nn.sigmoid(xg @ g1) @ g2
    heads = lambda t: t.reshape(B, T, N_HEADS, HEAD_SIZE)
    kk = heads((k * k_k).astype(f32))
    kk = kk / jnp.maximum(jnp.sqrt(jnp.sum(kk * kk, axis=-1, keepdims=True)), 1e-12)
    k = k.astype(f32) * (1 + (a - 1) * k_a.astype(f32))
    r_h, k_h, v_h, a_h = heads(r.astype(f32)), heads(k), heads(v.astype(f32)), heads(a)
    xs = tuple(jnp.moveaxis(t, 1, 0) for t in (r_h, heads(decay), k_h, v_h, -kk, kk * a_h))
    S_T, ys = lax.scan(_wkv_step, S0.astype(f32), xs)
    y = jnp.moveaxis(ys, 0, 1)
    mu = jnp.mean(y, axis=-1, keepdims=True)
    var = jnp.mean(jnp.square(y - mu), axis=-1, keepdims=True)
    y = ((y - mu) * lax.rsqrt(var + LN_X_EPS)).reshape(B, T, D) * lnx_w.astype(f32) + lnx_b.astype(f32)
    bonus = (jnp.sum(r_h * k_h * r_k.astype(f32), axis=-1, keepdims=True) * v_h).reshape(B, T, D)
    out = ((y + bonus).astype(h.dtype) * g) @ wo
    return out, h[:, -1], S_T, v_first


def _conformer_conv(h, buf, w_in, b_in, w_dw, b_dw, ln_w, ln_b, w_out, b_out):
    u = h @ w_in + b_in
    u = u[..., :D_CONV] * jax.nn.sigmoid(u[..., D_CONV:])
    up = jnp.concatenate([buf.astype(u.dtype), u], axis=1)
    z = lax.conv_general_dilated(up, w_dw[:, None, :], window_strides=(1,), padding='VALID',
                                 dimension_numbers=('NWC', 'WIO', 'NWC'),
                                 feature_group_count=D_CONV) + b_dw
    zf = z.astype(jnp.float32)
    mu = jnp.mean(zf, axis=-1, keepdims=True)
    var = jnp.mean(jnp.square(zf - mu), axis=-1, keepdims=True)
    zn = (zf - mu) * lax.rsqrt(var + CONV_LN_EPS) * ln_w.astype(jnp.float32) + ln_b.astype(jnp.float32)
    out = jax.nn.silu(zn).astype(h.dtype) @ w_out + b_out
    return out, up[:, -(CONV_WIDTH - 1):]


def _pool_mixer(h, buf, start_pos, w_grp, scale):
    B, T, D = h.shape
    xp = jnp.concatenate([buf.astype(h.dtype), h], axis=1)
    cs = jnp.cumsum(xp.astype(jnp.float32), axis=1)
    cs = jnp.concatenate([jnp.zeros((B, 1, D), jnp.float32), cs], axis=1)
    pos = start_pos + jnp.arange(T)
    end = cs[:, POOL_BUF + 1:]
    means = []
    for gi, win in enumerate(POOL_WINDOWS):
        sl = slice(gi * POOL_GROUP, (gi + 1) * POOL_GROUP)
        begin = cs[:, POOL_BUF + 1 - win:POOL_BUF + 1 - win + T, sl]
        cnt = jnp.minimum(pos + 1, win).astype(jnp.float32)[None, :, None]
        means.append((end[:, :, sl] - begin) / cnt)
    d = (jnp.concatenate(means, axis=-1) - h.astype(jnp.float32)).reshape(B, T, len(POOL_WINDOWS), POOL_GROUP)
    y = jnp.einsum('btgi,gio->btgo', d, w_grp.astype(jnp.float32)).reshape(B, T, D) * scale.astype(jnp.float32)
    return y.astype(h.dtype), xp[:, -POOL_BUF:]


def _moe(h, w_router, b_router, w_gu, b_gu, w_down, b_down):
    B, T, D = h.shape
    n = B * T
    x = h.reshape(n, D)
    logits = (x @ w_router + b_router).astype(jnp.float32)
    top_v, top_e = lax.top_k(logits, TOP_K)
    gates = jax.nn.softmax(top_v, axis=-1)
    n_assign = n * TOP_K
    flat_e = top_e.reshape(n_assign)
    flat_t = jnp.arange(n_assign) // TOP_K
    order = jnp.argsort(flat_e)
    e_sorted = flat_e[order]
    counts = jnp.bincount(flat_e, length=N_EXPERTS)
    padded = (counts + MOE_BLOCK - 1) // MOE_BLOCK * MOE_BLOCK
    start = jnp.cumsum(counts) - counts
    pend = jnp.cumsum(padded)
    pstart = pend - padded
    dest = pstart[e_sorted] + jnp.arange(n_assign) - start[e_sorted]
    n_blocks = -(-(n_assign + N_EXPERTS * (MOE_BLOCK - 1)) // MOE_BLOCK)
    n_rows = n_blocks * MOE_BLOCK
    row_tok = jnp.full((n_rows,), n, jnp.int32).at[dest].set(flat_t[order].astype(jnp.int32))
    row_gate = jnp.zeros((n_rows,), jnp.float32).at[dest].set(gates.reshape(n_assign)[order])
    blk_e = jnp.minimum(jnp.searchsorted(pend, jnp.arange(n_blocks) * MOE_BLOCK, side='right'), N_EXPERTS - 1)
    xpad = jnp.concatenate([x, jnp.zeros((1, D), x.dtype)], axis=0)
    xb = xpad[row_tok].reshape(n_blocks, MOE_BLOCK, D)

    def expert_block(args):
        xe, e = args
        gu = xe @ w_gu[e] + b_gu[e]
        g = jnp.minimum(gu[:, :D_EXPERT], SWIGLU_LIMIT)
        u = jnp.clip(gu[:, D_EXPERT:], -SWIGLU_LIMIT, SWIGLU_LIMIT)
        act = (u + 1) * g * jax.nn.sigmoid(SWIGLU_ALPHA * g)
        return act @ w_down[e] + b_down[e]

    yb = lax.map(expert_block, (xb, blk_e)).reshape(n_rows, D)
    out = jnp.zeros((n + 1, D), jnp.float32).at[row_tok].add(yb.astype(jnp.float32) * row_gate[:, None])
    return out[:n].astype(h.dtype).reshape(B, T, D)


def _trunk(x, c, start_pos, wkv_in, shift_in, conv_in, pool_in, p):
    wkv_out, shift_out, conv_out, pool_out = [], [], [], []
    v_first = None
    for i in range(DEPTH):
        sh1, sc1, gt1, sh2, sc2, gt2 = _adaln(c, p['ada_w'][i], p['ada_b'][i])
        h = _modulate(_rmsnorm(x, p['norm_mix'][i]), sh1, sc1)
        kind = i % N_MIXERS
        if kind == 0:
            j = len(wkv_out)
            vres = None if j == 0 else (p['rwkv_v0'][j - 1], p['rwkv_v1'][j - 1], p['rwkv_v2'][j - 1])
            out, last, S, v_first = _rwkv7(
                h, shift_in[j], wkv_in[j], v_first, vres, p['rwkv_mix'][j],
                p['rwkv_wr'][j], p['rwkv_wk'][j], p['rwkv_wv'][j], p['rwkv_wo'][j],
                p['rwkv_w0'][j], p['rwkv_w1'][j], p['rwkv_w2'][j],
                p['rwkv_a0'][j], p['rwkv_a1'][j], p['rwkv_a2'][j],
                p['rwkv_g1'][j], p['rwkv_g2'][j], p['rwkv_kk'][j], p['rwkv_ka'][j],
                p['rwkv_rk'][j], p['rwkv_lnx_w'][j], p['rwkv_lnx_b'][j])
            wkv_out.append(S)
            shift_out.append(last)
        elif kind == 1:
            j = len(conv_out)
            out, buf = _conformer_conv(
                h, conv_in[j], p['conv_w_in'][j], p['conv_b_in'][j], p['conv_w_dw'][j],
                p['conv_b_dw'][j], p['conv_ln_w'][j], p['conv_ln_b'][j],
                p['conv_w_out'][j], p['conv_b_out'][j])
            conv_out.append(buf)
        else:
            j = len(pool_out)
            out, buf = _pool_mixer(h, pool_in[j], start_pos, p['pool_w'][j], p['pool_scale'][j])
            pool_out.append(buf)
        x = x + gt1[:, None, :] * out
        hf = _modulate(_rmsnorm(x, p['norm_ffn'][i]), sh2, sc2)
        x = x + gt2[:, None, :] * _moe(hf, p['moe_w_router'][i], p['moe_b_router'][i], p['moe_w_gu'][i],
                                       p['moe_b_gu'][i], p['moe_w_down'][i], p['moe_b_down'][i])
    return (_rmsnorm(x, p['final_norm']), jnp.stack(wkv_out), jnp.stack(shift_out),
            jnp.stack(conv_out), jnp.stack(pool_out))


def setup_inputs(seed: int = 0) -> dict:
    key = jax.random.key(seed)
    ks = iter(jax.random.split(key, 64))

    def nrm(shape, s):
        return jax.random.normal(next(ks), shape, jnp.float32) * s

    def unif(shape, lo, hi):
        return jax.random.uniform(next(ks), shape, jnp.float32, lo, hi)

    D, H, N, E, F = D_MODEL, N_HEADS, HEAD_SIZE, N_EXPERTS, D_EXPERT
    sD = D ** -0.5
    return dict(
        x_prompt=nrm((BATCH, SEQ, D), 1.0),
        x_sample=nrm((DEC_BATCH, DEC_SEQ, D), 1.0),
        c_prompt=nrm((BATCH, D), 1.0),
        c_sample=nrm((DEC_BATCH, D), 1.0),
        state_wkv=nrm((N_A, DEC_BATCH, H, N, N), 0.3),
        state_shift=nrm((N_A, DEC_BATCH, D), 1.0),
        state_conv=nrm((N_B, DEC_BATCH, CONV_WIDTH - 1, D_CONV), 0.5),
        state_pool=nrm((N_C, DEC_BATCH, POOL_BUF, D), 1.0),
        ada_w=nrm((DEPTH, D, 6 * D), 0.5 * sD),
        ada_b=nrm((DEPTH, 6 * D), 0.05),
        norm_mix=1.0 + nrm((DEPTH, D), 0.1),
        norm_ffn=1.0 + nrm((DEPTH, D), 0.1),
        final_norm=1.0 + nrm((D,), 0.1),
        rwkv_mix=unif((N_A, 6, D), 0.0, 1.0),
        rwkv_wr=nrm((N_A, D, D), sD),
        rwkv_wk=nrm((N_A, D, D), sD),
        rwkv_wv=nrm((N_A, D, D), sD),
        rwkv_wo=nrm((N_A, D, D), sD),
        rwkv_w0=unif((N_A, D), -6.5, -1.5),
        rwkv_w1=nrm((N_A, D, D_DECAY_LORA), sD),
        rwkv_w2=nrm((N_A, D_DECAY_LORA, D), 0.5 * D_DECAY_LORA ** -0.5),
        rwkv_a0=nrm((N_A, D), 0.1),
        rwkv_a1=nrm((N_A, D, D_AAA_LORA), sD),
        rwkv_a2=nrm((N_A, D_AAA_LORA, D), D_AAA_LORA ** -0.5),
        rwkv_v0=nrm((N_A - 1, D), 0.1),
        rwkv_v1=nrm((N_A - 1, D, D_MV_LORA), sD),
        rwkv_v2=nrm((N_A - 1, D_MV_LORA, D), D_MV_LORA ** -0.5),
        rwkv_g1=nrm((N_A, D, D_GATE_LORA), sD),
        rwkv_g2=nrm((N_A, D_GATE_LORA, D), D_GATE_LORA ** -0.5),
        rwkv_kk=0.85 + nrm((N_A, D), 0.05),
        rwkv_ka=1.0 + nrm((N_A, D), 0.05),
        rwkv_rk=nrm((N_A, H, N), 0.1),
        rwkv_lnx_w=1.0 + nrm((N_A, D), 0.1),
        rwkv_lnx_b=nrm((N_A, D), 0.01),
        conv_w_in=nrm((N_B, D, 2 * D_CONV), sD),
        conv_b_in=nrm((N_B, 2 * D_CONV), 0.01),
        conv_w_dw=nrm((N_B, CONV_WIDTH, D_CONV), CONV_WIDTH ** -0.5),
        conv_b_dw=nrm((N_B, D_CONV), 0.01),
        conv_ln_w=1.0 + nrm((N_B, D_CONV), 0.1),
        conv_ln_b=nrm((N_B, D_CONV), 0.01),
        conv_w_out=nrm((N_B, D_CONV, D), D_CONV ** -0.5),
        conv_b_out=nrm((N_B, D), 0.01),
        pool_w=nrm((N_C, len(POOL_WINDOWS), POOL_GROUP, POOL_GROUP), POOL_GROUP ** -0.5),
        pool_scale=1.0 + nrm((N_C, D), 0.1),
        moe_w_router=nrm((DEPTH, D, E), sD),
        moe_b_router=nrm((DEPTH, E), 0.01),
        moe_w_gu=nrm((DEPTH, E, D, 2 * F), sD),
        moe_b_gu=nrm((DEPTH, E, 2 * F), 0.01),
        moe_w_down=nrm((DEPTH, E, F, D), F ** -0.5),
        moe_b_down=nrm((DEPTH, E, D), 0.01),
    )


def reference(x_prompt, x_sample, c_prompt, c_sample, state_wkv, state_shift, state_conv, state_pool,
              ada_w, ada_b, norm_mix, norm_ffn, final_norm,
              rwkv_mix, rwkv_wr, rwkv_wk, rwkv_wv, rwkv_wo, rwkv_w0, rwkv_w1, rwkv_w2,
              rwkv_a0, rwkv_a1, rwkv_a2, rwkv_v0, rwkv_v1, rwkv_v2, rwkv_g1, rwkv_g2,
              rwkv_kk, rwkv_ka, rwkv_rk, rwkv_lnx_w, rwkv_lnx_b,
              conv_w_in, conv_b_in, conv_w_dw, conv_b_dw, conv_ln_w, conv_ln_b, conv_w_out, conv_b_out,
              pool_w, pool_scale,
              moe_w_router, moe_b_router, moe_w_gu, moe_b_gu, moe_w_down, moe_b_down):
    p = dict(ada_w=ada_w, ada_b=ada_b, norm_mix=norm_mix, norm_ffn=norm_ffn, final_norm=final_norm,
             rwkv_mix=rwkv_mix, rwkv_wr=rwkv_wr, rwkv_wk=rwkv_wk, rwkv_wv=rwkv_wv, rwkv_wo=rwkv_wo,
             rwkv_w0=rwkv_w0, rwkv_w1=rwkv_w1, rwkv_w2=rwkv_w2, rwkv_a0=rwkv_a0, rwkv_a1=rwkv_a1,
             rwkv_a2=rwkv_a2, rwkv_v0=rwkv_v0, rwkv_v1=rwkv_v1, rwkv_v2=rwkv_v2, rwkv_g1=rwkv_g1,
             rwkv_g2=rwkv_g2, rwkv_kk=rwkv_kk, rwkv_ka=rwkv_ka, rwkv_rk=rwkv_rk,
             rwkv_lnx_w=rwkv_lnx_w, rwkv_lnx_b=rwkv_lnx_b,
             conv_w_in=conv_w_in, conv_b_in=conv_b_in, conv_w_dw=conv_w_dw, conv_b_dw=conv_b_dw,
             conv_ln_w=conv_ln_w, conv_ln_b=conv_ln_b, conv_w_out=conv_w_out, conv_b_out=conv_b_out,
             pool_w=pool_w, pool_scale=pool_scale,
             moe_w_router=moe_w_router, moe_b_router=moe_b_router, moe_w_gu=moe_w_gu,
             moe_b_gu=moe_b_gu, moe_w_down=moe_w_down, moe_b_down=moe_b_down)
    bp = x_prompt.shape[0]
    dt = x_prompt.dtype
    y_prompt, wkv_prompt, shift_prompt, conv_prompt, pool_prompt = _trunk(
        x_prompt, c_prompt, 0,
        jnp.zeros((N_A, bp, N_HEADS, HEAD_SIZE, HEAD_SIZE), jnp.float32),
        jnp.zeros((N_A, bp, D_MODEL), dt),
        jnp.zeros((N_B, bp, CONV_WIDTH - 1, D_CONV), dt),
        jnp.zeros((N_C, bp, POOL_BUF, D_MODEL), dt), p)
    y_sample, wkv_sample, shift_sample, conv_sample, pool_sample = _trunk(
        x_sample, c_sample, PAST_LEN, state_wkv, state_shift, state_conv, state_pool, p)
    return (y_prompt, y_sample, wkv_prompt, shift_prompt, conv_prompt, pool_prompt,
            wkv_sample, shift_sample, conv_sample, pool_sample)
```

```python
import functools

import jax
import jax.numpy as jnp
from jax import lax
from jax.experimental import pallas as pl
from jax.experimental.pallas import tpu as pltpu

F32 = jnp.float32
BF16 = jnp.bfloat16
HIGHEST = lax.Precision.HIGHEST

HEAD_SIZE = 64
TOP_K = 4
POOL_WINDOWS = (2, 4, 8, 16)
POOL_BUF = max(POOL_WINDOWS) - 1
PAST_LEN = 16384
RMS_EPS = 1e-5
LN_X_EPS = 64e-5
CONV_LN_EPS = 1e-5
SWIGLU_LIMIT = 7.0
SWIGLU_ALPHA = 1.702

VMEM_LIMIT_BYTES = 56 * 1024 * 1024
SUBLANES = 8
LANES = 128
WKV_CHUNK = 64
MOE_ROWS = 256


def _tile(n, pref, mult):
    if n <= pref:
        return n
    t = pref - pref % mult
    while t >= mult:
        if n % t == 0:
            return t
        t -= mult
    return n


def _params(*sem):
    return pltpu.CompilerParams(dimension_semantics=sem, vmem_limit_bytes=VMEM_LIMIT_BYTES)


def _bdot(a, b, dims=(((1,), (0,)), ((), ()))):
    return lax.dot_general(a.astype(BF16), b.astype(BF16), dims, preferred_element_type=F32)


_NT = (((1,), (1,)), ((), ()))
_TN = (((0,), (0,)), ((), ()))


def _apply_act(acc, act):
    if act == "tanh":
        return jnp.tanh(acc)
    if act == "sigmoid":
        return jax.nn.sigmoid(acc)
    assert act is None, act
    return acc


def _mm_body(x_ref, w_ref, b_ref, o_ref, wb_ref, *, act, exact):
    @pl.when(pl.program_id(1) == 0)
    def _():
        wb_ref[...] = w_ref[...].astype(wb_ref.dtype)

    if exact:
        acc = jnp.dot(x_ref[...].astype(F32), wb_ref[...], precision=HIGHEST, preferred_element_type=F32)
    else:
        acc = jnp.dot(x_ref[...].astype(BF16), wb_ref[...], preferred_element_type=F32)
    o_ref[...] = _apply_act(acc + b_ref[...], act).astype(o_ref.dtype)


def _matmul(x, w, b=None, layer=0, act=None, out_dtype=F32, exact=False, tm=512, tn=1024):
    m, k = x.shape
    n = w.shape[2]
    tm = _tile(m, tm, SUBLANES)
    tn = _tile(n, tn, LANES)
    b = jnp.zeros((1, n), F32) if b is None else b[layer].reshape(1, n)
    return pl.pallas_call(
        functools.partial(_mm_body, act=act, exact=exact),
        grid=(n // tn, m // tm),
        in_specs=[pl.BlockSpec((tm, k), lambda j, i: (i, 0)),
                  pl.BlockSpec((None, k, tn), lambda j, i: (layer, 0, j)),
                  pl.BlockSpec((1, tn), lambda j, i: (0, j))],
        out_specs=pl.BlockSpec((tm, tn), lambda j, i: (i, j)),
        out_shape=jax.ShapeDtypeStruct((m, n), out_dtype),
        scratch_shapes=[pltpu.VMEM((k, tn), F32 if exact else BF16)],
        compiler_params=_params("arbitrary", "arbitrary"),
        name="matmul",
    )(x, w, b)


def _glu_body(x_ref, wa_ref, wg_ref, ba_ref, bg_ref, o_ref, wab_ref, wgb_ref):
    @pl.when(pl.program_id(1) == 0)
    def _():
        wab_ref[...] = wa_ref[...].astype(BF16)
        wgb_ref[...] = wg_ref[...].astype(BF16)

    x = x_ref[...].astype(BF16)
    a = jnp.dot(x, wab_ref[...], preferred_element_type=F32) + ba_ref[...]
    g = jnp.dot(x, wgb_ref[...], preferred_element_type=F32) + bg_ref[...]
    o_ref[...] = (a * jax.nn.sigmoid(g)).astype(o_ref.dtype)


def _glu_matmul(x, w, b, layer, tm=512, tn=512):
    m, k = x.shape
    n = w.shape[2] // 2
    tm = _tile(m, tm, SUBLANES)
    tn = _tile(n, tn, LANES)
    nj = n // tn
    b = b[layer].reshape(1, 2 * n)
    return pl.pallas_call(
        _glu_body,
        grid=(nj, m // tm),
        in_specs=[pl.BlockSpec((tm, k), lambda j, i: (i, 0)),
                  pl.BlockSpec((None, k, tn), lambda j, i: (layer, 0, j)),
                  pl.BlockSpec((None, k, tn), lambda j, i: (layer, 0, j + nj)),
                  pl.BlockSpec((1, tn), lambda j, i: (0, j)),
                  pl.BlockSpec((1, tn), lambda j, i: (0, j + nj))],
        out_specs=pl.BlockSpec((tm, tn), lambda j, i: (i, j)),
        out_shape=jax.ShapeDtypeStruct((m, n), F32),
        scratch_shapes=[pltpu.VMEM((k, tn), BF16), pltpu.VMEM((k, tn), BF16)],
        compiler_params=_params("arbitrary", "arbitrary"),
        name="glu_matmul",
    )(x, w, w, b, b)


def _rms_mod(x, g, shift, scale):
    y = x * lax.rsqrt(jnp.mean(x * x, axis=-1, keepdims=True) + RMS_EPS)
    return (y * g) * (1.0 + scale) + shift


def _normmod_body(x_ref, g_ref, sh_ref, sc_ref, h_ref):
    h_ref[0] = _rms_mod(x_ref[0], g_ref[...], sh_ref[0], sc_ref[0]).astype(h_ref.dtype)


def _resnormmod_body(x_ref, y_ref, gt_ref, g_ref, sh_ref, sc_ref, xo_ref, h_ref):
    x = x_ref[0] + gt_ref[0] * y_ref[0]
    xo_ref[0] = x
    h_ref[0] = _rms_mod(x, g_ref[...], sh_ref[0], sc_ref[0]).astype(h_ref.dtype)


def _resnorm_body(x_ref, y_ref, gt_ref, g_ref, o_ref):
    x = x_ref[0] + gt_ref[0] * y_ref[0]
    o_ref[0] = (x * lax.rsqrt(jnp.mean(x * x, axis=-1, keepdims=True) + RMS_EPS)) * g_ref[...]


def _mod_spec(mod, tt):
    if mod.shape[1] == 1:
        return pl.BlockSpec((1, 1, mod.shape[2]), lambda b, t: (b, 0, 0))
    return pl.BlockSpec((1, tt, mod.shape[2]), lambda b, t: (b, t, 0))


def _normmod(x, g, shift, scale, out_dtype):
    bsz, t, d = x.shape
    tt = _tile(t, 512, SUBLANES)
    row = pl.BlockSpec((1, tt, d), lambda b, i: (b, i, 0))
    return pl.pallas_call(
        _normmod_body,
        grid=(bsz, t // tt),
        in_specs=[row, pl.BlockSpec((1, d), lambda b, i: (0, 0)), _mod_spec(shift, tt), _mod_spec(scale, tt)],
        out_specs=row,
        out_shape=jax.ShapeDtypeStruct((bsz, t, d), out_dtype),
        compiler_params=_params("arbitrary", "arbitrary"),
        name="normmod",
    )(x, g.reshape(1, d), shift, scale)


def _resnormmod(x, y, gate, g, shift, scale, out_dtype):
    bsz, t, d = x.shape
    tt = _tile(t, 512, SUBLANES)
    row = pl.BlockSpec((1, tt, d), lambda b, i: (b, i, 0))
    return pl.pallas_call(
        _resnormmod_body,
        grid=(bsz, t // tt),
        in_specs=[row, row, _mod_spec(gate, tt), pl.BlockSpec((1, d), lambda b, i: (0, 0)),
                  _mod_spec(shift, tt), _mod_spec(scale, tt)],
        out_specs=[row, row],
        out_shape=[jax.ShapeDtypeStruct((bsz, t, d), F32), jax.ShapeDtypeStruct((bsz, t, d), out_dtype)],
        compiler_params=_params("arbitrary", "arbitrary"),
        name="resnormmod",
    )(x, y, gate, g.reshape(1, d), shift, scale)


def _resnorm(x, y, gate, g):
    bsz, t, d = x.shape
    tt = _tile(t, 512, SUBLANES)
    row = pl.BlockSpec((1, tt, d), lambda b, i: (b, i, 0))
    return pl.pallas_call(
        _resnorm_body,
        grid=(bsz, t // tt),
        in_specs=[row, row, _mod_spec(gate, tt), pl.BlockSpec((1, d), lambda b, i: (0, 0))],
        out_specs=row,
        out_shape=jax.ShapeDtypeStruct((bsz, t, d), F32),
        compiler_params=_params("arbitrary", "arbitrary"),
        name="resnorm",
    )(x, y, gate, g.reshape(1, d))


def _wkv_body(r_ref, lw_ref, k_ref, v_ref, a_ref, b_ref, s0_ref, y_ref, st_ref, s_scr, *, chunk, pairs, n_chunks,
              n_levels):
    tstep = pl.program_id(2)

    @pl.when(tstep == 0)
    def _():
        s_scr[...] = s0_ref[0]

    rows = 2 * chunk
    lane = lax.broadcasted_iota(jnp.int32, (chunk, LANES), 1)
    head0 = (lane < HEAD_SIZE).astype(F32)
    head1 = 1.0 - head0
    ri = lax.broadcasted_iota(jnp.int32, (rows, rows), 0)
    ci = lax.broadcasted_iota(jnp.int32, (rows, rows), 1)
    strict = ri > ci
    incl = ri >= ci
    eye = (ri == ci).astype(F32)
    merge = [strict & ((ri >> (q + 1)) == (ci >> (q + 1))) & ((ri >> q) != (ci >> q)) for q in range(n_levels)]
    ti = lax.broadcasted_iota(jnp.int32, (chunk, chunk), 0)
    tj = lax.broadcasted_iota(jnp.int32, (chunk, chunk), 1)
    tri = (ti >= tj).astype(F32)

    def stack2(x):
        return jnp.concatenate([x * head0, x * head1], axis=0)

    def one_chunk(c, carry):
        sl = pl.ds(pl.multiple_of(c * chunk, chunk), chunk)
        for p in range(pairs):
            ls = slice(p * LANES, (p + 1) * LANES)
            lw = lw_ref[0, sl, ls]
            cl = jnp.dot(tri, lw, precision=HIGHEST, preferred_element_type=F32)
            c_in = jnp.exp(cl)
            c_inv = jnp.exp(-cl)
            c_ex = jnp.exp(cl - lw)
            c_end = c_in[chunk - 1:chunk, :]
            rt = stack2(r_ref[0, sl, ls] * c_in).astype(BF16)
            at = stack2(a_ref[0, sl, ls] * c_ex).astype(BF16)
            bt = stack2(b_ref[0, sl, ls] * c_inv).astype(BF16)
            kt = stack2(k_ref[0, sl, ls] * c_inv).astype(BF16)
            v2 = stack2(v_ref[0, sl, ls]).astype(BF16)
            s = s_scr[p]
            sb = s.astype(BF16)
            l_ab = jnp.where(strict, _bdot(at, bt, _NT), 0.0)
            l_ak = jnp.where(strict, _bdot(at, kt, _NT), 0.0)
            m_rb = jnp.where(incl, _bdot(rt, bt, _NT), 0.0)
            m_rk = jnp.where(incl, _bdot(rt, kt, _NT), 0.0)
            tinv = eye + jnp.where(merge[0], l_ab, 0.0)
            for lv in range(1, n_levels):
                tb = tinv.astype(BF16)
                tinv = tinv + _bdot(tb, _bdot(jnp.where(merge[lv], l_ab, 0.0), tb))
            u = _bdot(tinv, _bdot(at, sb, _NT) + _bdot(l_ak, v2))
            y2 = _bdot(rt, sb, _NT) + _bdot(m_rb, u) + _bdot(m_rk, v2)
            y_ref[0, sl, ls] = y2[:chunk] + y2[chunk:]
            upd = _bdot(u, bt, _TN) + _bdot(v2, kt, _TN)
            s_scr[p] = (s + upd) * c_end
        return carry

    lax.fori_loop(0, n_chunks, one_chunk, 0)

    @pl.when(tstep == pl.num_programs(2) - 1)
    def _():
        st_ref[0] = s_scr[...]


def _pairs_to_blockdiag(s):
    bsz, h, n, _ = s.shape
    s5 = s.reshape(bsz, h // 2, 2, n, n)
    out = jnp.einsum("bpivk,ij->bpivjk", s5, jnp.eye(2, dtype=s.dtype))
    return out.reshape(bsz, h // 2, 2 * n, 2 * n)


def _blockdiag_to_pairs(s2):
    bsz, hp, n2, _ = s2.shape
    n = n2 // 2
    s6 = s2.reshape(bsz, hp, 2, n, 2, n)
    return jnp.stack([s6[:, :, 0, :, 0, :], s6[:, :, 1, :, 1, :]], axis=2).reshape(bsz, 2 * hp, n, n)


def _wkv(r, lw, k, v, a, b, s0):
    bsz, t, d = r.shape
    t_real = t
    if t % SUBLANES:
        pad = SUBLANES - t % SUBLANES
        r, lw, k, v, a, b = (jnp.pad(z, ((0, 0), (0, pad), (0, 0))) for z in (r, lw, k, v, a, b))
        t += pad
    chunk = _tile(t, WKV_CHUNK, SUBLANES)
    tt = _tile(t, 512, chunk)
    n_pairs = d // LANES
    pairs = _tile(n_pairs, 4, 1)
    n_levels = (chunk - 1).bit_length()
    assert chunk == 1 << n_levels, "the block-merge inverse needs a power-of-two chunk"
    seq = pl.BlockSpec((1, tt, pairs * LANES), lambda i, p, j: (i, j, p))
    st = pl.BlockSpec((1, pairs, LANES, LANES), lambda i, p, j: (i, p, 0, 0))
    y, s_t = pl.pallas_call(
        functools.partial(_wkv_body, chunk=chunk, pairs=pairs, n_chunks=tt // chunk, n_levels=n_levels),
        grid=(bsz, n_pairs // pairs, t // tt),
        in_specs=[seq] * 6 + [st],
        out_specs=[seq, st],
        out_shape=[jax.ShapeDtypeStruct((bsz, t, d), F32),
                   jax.ShapeDtypeStruct((bsz, n_pairs, LANES, LANES), F32)],
        scratch_shapes=[pltpu.VMEM((pairs, LANES, LANES), F32)],
        compiler_params=_params("arbitrary", "arbitrary", "arbitrary"),
        name="wkv",
    )(r, lw, k, v, a, b, _pairs_to_blockdiag(s0))
    return y[:, :t_real], _blockdiag_to_pairs(s_t)


def _dwconv_body(u_ref, buf_ref, w_ref, b_ref, lnw_ref, lnb_ref, o_ref, win_ref, *, tt, width, halo):
    @pl.when(pl.program_id(1) == 0)
    def _():
        win_ref[0:halo, :] = buf_ref[0]

    win_ref[halo:halo + tt, :] = u_ref[0]
    first = halo - (width - 1)
    z = b_ref[...] + w_ref[0:1, :] * win_ref[first:first + tt, :]
    for j in range(1, width):
        z = z + w_ref[j:j + 1, :] * win_ref[first + j:first + j + tt, :]
    mu = jnp.mean(z, axis=-1, keepdims=True)
    zc = z - mu
    var = jnp.mean(zc * zc, axis=-1, keepdims=True)
    zn = zc * lax.rsqrt(var + CONV_LN_EPS) * lnw_ref[...] + lnb_ref[...]
    o_ref[0] = (zn * jax.nn.sigmoid(zn)).astype(o_ref.dtype)
    win_ref[0:halo, :] = win_ref[tt:tt + halo, :]


def _dwconv_ln_silu(u, buf, w_dw, b_dw, ln_w, ln_b):
    bsz, t, d = u.shape
    width = w_dw.shape[0]
    halo = -(-(width - 1) // SUBLANES) * SUBLANES
    tt = _tile(t, 128, SUBLANES)
    assert tt >= halo or tt == t
    bufp = jnp.pad(buf, ((0, 0), (halo - (width - 1), 0), (0, 0)))
    vec = pl.BlockSpec((1, d), lambda i, j: (0, 0))
    return pl.pallas_call(
        functools.partial(_dwconv_body, tt=tt, width=width, halo=halo),
        grid=(bsz, t // tt),
        in_specs=[pl.BlockSpec((1, tt, d), lambda i, j: (i, j, 0)),
                  pl.BlockSpec((1, halo, d), lambda i, j: (i, 0, 0)),
                  pl.BlockSpec((width, d), lambda i, j: (0, 0)), vec, vec, vec],
        out_specs=pl.BlockSpec((1, tt, d), lambda i, j: (i, j, 0)),
        out_shape=jax.ShapeDtypeStruct((bsz, t, d), BF16),
        scratch_shapes=[pltpu.VMEM((halo + max(tt, halo), d), F32)],
        compiler_params=_params("arbitrary", "arbitrary"),
        name="dwconv_ln_silu",
    )(u, bufp, w_dw, b_dw.reshape(1, d), ln_w.reshape(1, d), ln_b.reshape(1, d))


def _pool_body(h_ref, buf_ref, w_ref, sc_ref, o_ref, win_ref, wb_ref, *, tt, halo, start_pos, group):
    tstep = pl.program_id(1)

    @pl.when((pl.program_id(0) == 0) & (tstep == 0))
    def _():
        wb_ref[...] = w_ref[...].astype(BF16)

    @pl.when(tstep == 0)
    def _():
        win_ref[0:halo, :] = buf_ref[0]

    win_ref[halo:halo + tt, :] = h_ref[0]
    pos = start_pos + tstep * tt + lax.broadcasted_iota(jnp.int32, (tt, 1), 0)
    for gi, win in enumerate(POOL_WINDOWS):
        cs = slice(gi * group, (gi + 1) * group)
        tot = win_ref[halo:halo + tt, cs]
        for back in range(1, win):
            tot = tot + win_ref[halo - back:halo - back + tt, cs]
        cnt = jnp.minimum(pos + 1, win).astype(F32)
        dlt = tot / cnt - win_ref[halo:halo + tt, cs]
        o_ref[0, :, cs] = jnp.dot(dlt.astype(BF16), wb_ref[gi], preferred_element_type=F32) * sc_ref[:, cs]
    win_ref[0:halo, :] = win_ref[tt:tt + halo, :]


def _pool_mixer(h, buf, start_pos, w_grp, scale):
    bsz, t, d = h.shape
    ng, group, _ = w_grp.shape
    halo = -(-POOL_BUF // SUBLANES) * SUBLANES
    tt = _tile(t, 256, SUBLANES)
    assert tt >= halo or tt == t
    bufp = jnp.pad(buf, ((0, 0), (halo - POOL_BUF, 0), (0, 0)))
    return pl.pallas_call(
        functools.partial(_pool_body, tt=tt, halo=halo, start_pos=start_pos, group=group),
        grid=(bsz, t // tt),
        in_specs=[pl.BlockSpec((1, tt, d), lambda i, j: (i, j, 0)),
                  pl.BlockSpec((1, halo, d), lambda i, j: (i, 0, 0)),
                  pl.BlockSpec((ng, group, group), lambda i, j: (0, 0, 0)),
                  pl.BlockSpec((1, d), lambda i, j: (0, 0))],
        out_specs=pl.BlockSpec((1, tt, d), lambda i, j: (i, j, 0)),
        out_shape=jax.ShapeDtypeStruct((bsz, t, d), F32),
        scratch_shapes=[pltpu.VMEM((halo + max(tt, halo), d), F32), pltpu.VMEM((ng, group, group), BF16)],
        compiler_params=_params("arbitrary", "arbitrary"),
        name="pool_mixer",
    )(h, bufp, w_grp, scale.reshape(1, d))


def _expert_changed(be_ref, i):
    return (i == 0) | (be_ref[i] != be_ref[jnp.maximum(i - 1, 0)])


def _moe_up_body(be_ref, nb_ref, x_ref, wg_ref, wu_ref, bg_ref, bu_ref, h_ref, wgb_ref, wub_ref):
    i = pl.program_id(1)

    @pl.when(_expert_changed(be_ref, i))
    def _():
        wgb_ref[...] = wg_ref[...].astype(BF16)
        wub_ref[...] = wu_ref[...].astype(BF16)

    @pl.when(i < nb_ref[0])
    def _():
        x = x_ref[...]
        g = jnp.dot(x, wgb_ref[...], preferred_element_type=F32) + bg_ref[...]
        u = jnp.dot(x, wub_ref[...], preferred_element_type=F32) + bu_ref[...]
        g = jnp.minimum(g, SWIGLU_LIMIT)
        u = jnp.clip(u, -SWIGLU_LIMIT, SWIGLU_LIMIT)
        h_ref[...] = ((u + 1.0) * g * jax.nn.sigmoid(SWIGLU_ALPHA * g)).astype(h_ref.dtype)

    @pl.when(i >= nb_ref[0])
    def _():
        h_ref[...] = jnp.zeros_like(h_ref)


def _moe_down_body(be_ref, nb_ref, h_ref, w_ref, b_ref, gate_ref, y_ref, wb_ref):
    i = pl.program_id(1)

    @pl.when(_expert_changed(be_ref, i))
    def _():
        wb_ref[...] = w_ref[...].astype(BF16)

    @pl.when(i < nb_ref[0])
    def _():
        y = jnp.dot(h_ref[...], wb_ref[...], preferred_element_type=F32) + b_ref[...]
        y_ref[...] = y * gate_ref[...]

    @pl.when(i >= nb_ref[0])
    def _():
        y_ref[...] = jnp.zeros_like(y_ref)


def _moe_experts(xs, blk_e, n_used, row_gate, w_gu, b_gu, w_down, b_down, layer):
    n_rows, d = xs.shape
    _, n_exp, _, f2 = w_gu.shape
    f = f2 // 2
    bm = MOE_ROWS
    n_blocks = n_rows // bm
    tf = _tile(f, 512, LANES)
    nf = f // tf
    b_gu3 = b_gu[layer].reshape(n_exp, 1, f2)
    hidden = pl.pallas_call(
        _moe_up_body,
        grid_spec=pltpu.PrefetchScalarGridSpec(
            num_scalar_prefetch=2,
            grid=(nf, n_blocks),
            in_specs=[pl.BlockSpec((bm, d), lambda j, i, be, nb: (i, 0)),
                      pl.BlockSpec((None, None, d, tf), lambda j, i, be, nb: (layer, be[i], 0, j)),
                      pl.BlockSpec((None, None, d, tf), lambda j, i, be, nb: (layer, be[i], 0, j + nf)),
                      pl.BlockSpec((None, 1, tf), lambda j, i, be, nb: (be[i], 0, j)),
                      pl.BlockSpec((None, 1, tf), lambda j, i, be, nb: (be[i], 0, j + nf))],
            out_specs=pl.BlockSpec((bm, tf), lambda j, i, be, nb: (i, j)),
            scratch_shapes=[pltpu.VMEM((d, tf), BF16), pltpu.VMEM((d, tf), BF16)]),
        out_shape=jax.ShapeDtypeStruct((n_rows, f), BF16),
        compiler_params=_params("arbitrary", "arbitrary"),
        name="moe_up",
    )(blk_e, n_used, xs, w_gu, w_gu, b_gu3, b_gu3)
    tn = _tile(d, 512, LANES)
    return pl.pallas_call(
        _moe_down_body,
        grid_spec=pltpu.PrefetchScalarGridSpec(
            num_scalar_prefetch=2,
            grid=(d // tn, n_blocks),
            in_specs=[pl.BlockSpec((bm, f), lambda j, i, be, nb: (i, 0)),
                      pl.BlockSpec((None, None, f, tn), lambda j, i, be, nb: (layer, be[i], 0, j)),
                      pl.BlockSpec((None, 1, tn), lambda j, i, be, nb: (be[i], 0, j)),
                      pl.BlockSpec((bm, 1), lambda j, i, be, nb: (i, 0))],
            out_specs=pl.BlockSpec((bm, tn), lambda j, i, be, nb: (i, j)),
            scratch_shapes=[pltpu.VMEM((f, tn), BF16)]),
        out_shape=jax.ShapeDtypeStruct((n_rows, d), F32),
        compiler_params=_params("arbitrary", "arbitrary"),
        name="moe_down",
    )(blk_e, n_used, hidden, w_down, b_down[layer].reshape(n_exp, 1, d), row_gate.reshape(n_rows, 1))


def _moe(hf, w_router, b_router, w_gu, b_gu, w_down, b_down, layer):
    n, d = hf.shape
    n_exp = w_router.shape[2]
    bm = MOE_ROWS
    logits = _matmul(hf, w_router, b_router, layer, exact=True)
    top_v, top_e = lax.top_k(logits, TOP_K)
    gates = jax.nn.softmax(top_v, axis=-1).reshape(-1)
    n_assign = n * TOP_K
    flat_e = top_e.reshape(n_assign).astype(jnp.int32)
    onehot = (flat_e[:, None] == jnp.arange(n_exp, dtype=jnp.int32)[None, :]).astype(jnp.int32)
    running = jnp.cumsum(onehot, axis=0)
    counts = running[-1]
    rank = jnp.take_along_axis(running, flat_e[:, None], axis=1)[:, 0] - 1
    padded = (counts + bm - 1) // bm * bm
    pend = jnp.cumsum(padded)
    pstart = pend - padded
    start = jnp.cumsum(counts) - counts
    pos = pstart[flat_e] + rank
    n_blocks = -(-(n_assign + n_exp * (bm - 1)) // bm)
    n_rows = n_blocks * bm
    blk_e = jnp.minimum(jnp.searchsorted(pend, jnp.arange(n_blocks, dtype=jnp.int32) * bm, side="right"),
                        n_exp - 1).astype(jnp.int32)
    n_used = (pend[-1] // bm).astype(jnp.int32).reshape(1)
    order = jnp.argsort(flat_e, stable=True).astype(jnp.int32)
    row = jnp.arange(n_rows, dtype=jnp.int32)
    row_e = blk_e[row // bm]
    row_rank = row - pstart[row_e]
    valid = row_rank < counts[row_e]
    src = order[jnp.clip(start[row_e] + row_rank, 0, n_assign - 1)]
    row_tok = jnp.where(valid, src // TOP_K, 0)
    row_gate = jnp.where(valid, gates[src], 0.0)
    xs = hf.astype(BF16)[row_tok]
    y = _moe_experts(xs, blk_e, n_used, row_gate, w_gu, b_gu, w_down, b_down, layer)
    return y[pos].reshape(n, TOP_K, d).sum(axis=1)


def _rwkv7(h, shift_prev, s0, v_first, p, j):
    bsz, t, d = h.shape
    n = bsz * t
    heads = d // HEAD_SIZE
    prev = jnp.concatenate([shift_prev[:, None], h[:, :-1]], axis=1)
    xx = prev - h
    mix = p["rwkv_mix"][j]
    xr, xw, xk, xv, xa, xg = ((h + xx * mix[m]).astype(BF16).reshape(n, d) for m in range(6))
    r = _matmul(xr, p["rwkv_wr"], layer=j)
    k = _matmul(xk, p["rwkv_wk"], layer=j)
    v = _matmul(xv, p["rwkv_wv"], layer=j)
    if j == 0:
        v_first = v
    else:
        v_lora = _matmul(_matmul(xv, p["rwkv_v1"], layer=j - 1, out_dtype=BF16), p["rwkv_v2"], layer=j - 1)
        v = v + (v_first - v) * jax.nn.sigmoid(p["rwkv_v0"][j - 1] + v_lora)
    w_lora = _matmul(_matmul(xw, p["rwkv_w1"], layer=j, act="tanh", out_dtype=BF16), p["rwkv_w2"], layer=j)
    logw = -jax.nn.softplus(-(p["rwkv_w0"][j] + w_lora)) - 0.5
    log_decay = -jnp.exp(logw)
    a_lora = _matmul(_matmul(xa, p["rwkv_a1"], layer=j, out_dtype=BF16), p["rwkv_a2"], layer=j)
    a = jax.nn.sigmoid(p["rwkv_a0"][j] + a_lora)
    g = _matmul(_matmul(xg, p["rwkv_g1"], layer=j, act="sigmoid", out_dtype=BF16), p["rwkv_g2"], layer=j)
    kk = (k * p["rwkv_kk"][j]).reshape(n, heads, HEAD_SIZE)
    kk = kk / jnp.maximum(jnp.sqrt(jnp.sum(kk * kk, axis=-1, keepdims=True)), 1e-12)
    kk = kk.reshape(n, d)
    k = k * (1.0 + (a - 1.0) * p["rwkv_ka"][j])
    seq = lambda z: z.reshape(bsz, t, d)
    y, s_t = _wkv(seq(r), seq(log_decay), seq(k), seq(v), seq(-kk), seq(kk * a), s0)
    y = y.reshape(n, heads, HEAD_SIZE)
    mu = jnp.mean(y, axis=-1, keepdims=True)
    var = jnp.mean(jnp.square(y - mu), axis=-1, keepdims=True)
    y = ((y - mu) * lax.rsqrt(var + LN_X_EPS)).reshape(n, d) * p["rwkv_lnx_w"][j] + p["rwkv_lnx_b"][j]
    rk = (r * k).reshape(n, heads, HEAD_SIZE) * p["rwkv_rk"][j]
    bonus = (jnp.sum(rk, axis=-1, keepdims=True) * v.reshape(n, heads, HEAD_SIZE)).reshape(n, d)
    out = _matmul(((y + bonus) * g).astype(BF16), p["rwkv_wo"], layer=j)
    return out.reshape(bsz, t, d), h[:, -1], s_t, v_first


def _conformer_conv(h, buf, p, j):
    bsz, t, d = h.shape
    width = p["conv_w_dw"].shape[1]
    u = _glu_matmul(h.reshape(bsz * t, d), p["conv_w_in"], p["conv_b_in"], j).reshape(bsz, t, -1)
    z = _dwconv_ln_silu(u, buf, p["conv_w_dw"][j], p["conv_b_dw"][j], p["conv_ln_w"][j], p["conv_ln_b"][j])
    out = _matmul(z.reshape(bsz * t, -1), p["conv_w_out"], p["conv_b_out"], j).reshape(bsz, t, d)
    new_buf = jnp.concatenate([buf, u], axis=1)[:, -(width - 1):]
    return out, new_buf


def _pool(h, buf, start_pos, p, j):
    out = _pool_mixer(h, buf, start_pos, p["pool_w"][j], p["pool_scale"][j])
    return out, jnp.concatenate([buf, h], axis=1)[:, -POOL_BUF:]


def kernel(x_prompt, x_sample, c_prompt, c_sample, state_wkv, state_shift, state_conv, state_pool, ada_w, ada_b, norm_mix, norm_ffn, final_norm, rwkv_mix, rwkv_wr, rwkv_wk, rwkv_wv, rwkv_wo, rwkv_w0, rwkv_w1, rwkv_w2, rwkv_a0, rwkv_a1, rwkv_a2, rwkv_v0, rwkv_v1, rwkv_v2, rwkv_g1, rwkv_g2, rwkv_kk, rwkv_ka, rwkv_rk, rwkv_lnx_w, rwkv_lnx_b, conv_w_in, conv_b_in, conv_w_dw, conv_b_dw, conv_ln_w, conv_ln_b, conv_w_out, conv_b_out, pool_w, pool_scale, moe_w_router, moe_b_router, moe_w_gu, moe_b_gu, moe_w_down, moe_b_down):
    p = dict(rwkv_mix=rwkv_mix, rwkv_wr=rwkv_wr, rwkv_wk=rwkv_wk, rwkv_wv=rwkv_wv, rwkv_wo=rwkv_wo,
             rwkv_w0=rwkv_w0, rwkv_w1=rwkv_w1, rwkv_w2=rwkv_w2, rwkv_a0=rwkv_a0, rwkv_a1=rwkv_a1,
             rwkv_a2=rwkv_a2, rwkv_v0=rwkv_v0, rwkv_v1=rwkv_v1, rwkv_v2=rwkv_v2, rwkv_g1=rwkv_g1,
             rwkv_g2=rwkv_g2, rwkv_kk=rwkv_kk, rwkv_ka=rwkv_ka, rwkv_rk=rwkv_rk, rwkv_lnx_w=rwkv_lnx_w,
             rwkv_lnx_b=rwkv_lnx_b, conv_w_in=conv_w_in, conv_b_in=conv_b_in, conv_w_dw=conv_w_dw,
             conv_b_dw=conv_b_dw, conv_ln_w=conv_ln_w, conv_ln_b=conv_ln_b, conv_w_out=conv_w_out,
             conv_b_out=conv_b_out, pool_w=pool_w, pool_scale=pool_scale)
    depth = ada_w.shape[0]
    bp, tp, d = x_prompt.shape
    bs, ts, _ = x_sample.shape
    heads = d // HEAD_SIZE
    n_p, n_s = bp * tp, bs * ts
    width = conv_w_dw.shape[1]

    c_act = jax.nn.silu(jnp.concatenate([c_prompt, c_sample], axis=0))
    mods = []
    for i in range(depth):
        m = _matmul(c_act, ada_w, ada_b, i)
        six = jnp.split(m, 6, axis=-1)
        mods.append(([z[:bp, None, :] for z in six],
                     [jnp.repeat(z[bp:], ts, axis=0)[None] for z in six]))

    groups = [
        dict(x=x_prompt, b=bp, t=tp, start=0,
             wkv=jnp.zeros((state_wkv.shape[0], bp, heads, HEAD_SIZE, HEAD_SIZE), F32),
             shift=jnp.zeros((state_shift.shape[0], bp, d), F32),
             conv=jnp.zeros((state_conv.shape[0], bp, width - 1, conv_w_dw.shape[2]), F32),
             pool=jnp.zeros((state_pool.shape[0], bp, POOL_BUF, d), F32)),
        dict(x=x_sample.reshape(1, n_s, d), b=bs, t=ts, start=PAST_LEN,
             wkv=state_wkv, shift=state_shift, conv=state_conv, pool=state_pool),
    ]
    for gi, grp in enumerate(groups):
        grp.update(wkv_out=[], shift_out=[], conv_out=[], pool_out=[], v_first=None)
        sh1, sc1 = mods[0][gi][0], mods[0][gi][1]
        grp["h"] = _normmod(grp["x"], norm_mix[0], sh1, sc1, F32)

    for i in range(depth):
        kind = i % 3
        hfs = []
        for gi, grp in enumerate(groups):
            sh1, sc1, gt1, sh2, sc2, gt2 = mods[i][gi]
            h = grp["h"].reshape(grp["b"], grp["t"], d)
            if kind == 0:
                j = len(grp["wkv_out"])
                out, last, s_t, grp["v_first"] = _rwkv7(h, grp["shift"][j], grp["wkv"][j], grp["v_first"], p, j)
                grp["wkv_out"].append(s_t)
                grp["shift_out"].append(last)
            elif kind == 1:
                j = len(grp["conv_out"])
                out, buf = _conformer_conv(h, grp["conv"][j], p, j)
                grp["conv_out"].append(buf)
            else:
                j = len(grp["pool_out"])
                out, buf = _pool(h, grp["pool"][j], grp["start"], p, j)
                grp["pool_out"].append(buf)
            grp["x"], hf = _resnormmod(grp["x"], out.reshape(grp["x"].shape), gt1, norm_ffn[i], sh2, sc2, F32)
            hfs.append(hf.reshape(-1, d))
        moe = _moe(jnp.concatenate(hfs, axis=0), moe_w_router, moe_b_router, moe_w_gu, moe_b_gu, moe_w_down,
                   moe_b_down, i)
        for gi, grp in enumerate(groups):
            gt2 = mods[i][gi][5]
            y = (moe[:n_p] if gi == 0 else moe[n_p:]).reshape(grp["x"].shape)
            if i + 1 < depth:
                sh1, sc1 = mods[i + 1][gi][0], mods[i + 1][gi][1]
                grp["x"], grp["h"] = _resnormmod(grp["x"], y, gt2, norm_mix[i + 1], sh1, sc1, F32)
            else:
                grp["x"] = _resnorm(grp["x"], y, gt2, final_norm)

    gp, gs = groups
    return (gp["x"], gs["x"].reshape(bs, ts, d),
            jnp.stack(gp["wkv_out"]), jnp.stack(gp["shift_out"]), jnp.stack(gp["conv_out"]), jnp.stack(gp["pool_out"]),
            jnp.stack(gs["wkv_out"]), jnp.stack(gs["shift_out"]), jnp.stack(gs["conv_out"]), jnp.stack(gs["pool_out"]))
```

```python
import functools

import jax
import jax.numpy as jnp
from jax import lax
from jax.experimental import pallas as pl
from jax.experimental.pallas import tpu as pltpu

F32 = jnp.float32
BF16 = jnp.bfloat16
HIGHEST = lax.Precision.HIGHEST

HEAD_SIZE = 64
TOP_K = 4
POOL_WINDOWS = (2, 4, 8, 16)
POOL_BUF = max(POOL_WINDOWS) - 1
PAST_LEN = 16384
RMS_EPS = 1e-5
LN_X_EPS = 64e-5
CONV_LN_EPS = 1e-5
SWIGLU_LIMIT = 7.0
SWIGLU_ALPHA = 1.702

VMEM_LIMIT_BYTES = 56 * 1024 * 1024
SUBLANES = 8
LANES = 128
WKV_CHUNK = 64
WKV_PAIRS = 16
WKV_TIME_TILE = 128
MOE_ROWS = 256


def _tile(n, pref, mult):
    if n <= pref:
        return n
    t = pref - pref % mult
    while t >= mult:
        if n % t == 0:
            return t
        t -= mult
    return n


def _params(*sem):
    return pltpu.CompilerParams(dimension_semantics=sem, vmem_limit_bytes=VMEM_LIMIT_BYTES)


def _bdot(a, b, dims=(((1,), (0,)), ((), ()))):
    return lax.dot_general(a.astype(BF16), b.astype(BF16), dims, preferred_element_type=F32)


_NT = (((1,), (1,)), ((), ()))
_TN = (((0,), (0,)), ((), ()))


def _apply_act(acc, act):
    if act == "tanh":
        return jnp.tanh(acc)
    if act == "sigmoid":
        return jax.nn.sigmoid(acc)
    assert act is None, act
    return acc


def _mm_body(x_ref, w_ref, b_ref, o_ref, wb_ref, *, act, exact):
    @pl.when(pl.program_id(1) == 0)
    def _():
        wb_ref[...] = w_ref[...].astype(wb_ref.dtype)

    if exact:
        acc = jnp.dot(x_ref[...].astype(F32), wb_ref[...], precision=HIGHEST, preferred_element_type=F32)
    else:
        acc = jnp.dot(x_ref[...].astype(BF16), wb_ref[...], preferred_element_type=F32)
    o_ref[...] = _apply_act(acc + b_ref[...], act).astype(o_ref.dtype)


def _matmul(x, w, b=None, layer=0, act=None, out_dtype=F32, exact=False, tm=512, tn=1024):
    m, k = x.shape
    n = w.shape[2]
    tm = _tile(m, tm, SUBLANES)
    tn = _tile(n, tn, LANES)
    b = jnp.zeros((1, n), F32) if b is None else b[layer].reshape(1, n)
    return pl.pallas_call(
        functools.partial(_mm_body, act=act, exact=exact),
        grid=(n // tn, m // tm),
        in_specs=[pl.BlockSpec((tm, k), lambda j, i: (i, 0)),
                  pl.BlockSpec((None, k, tn), lambda j, i: (layer, 0, j)),
                  pl.BlockSpec((1, tn), lambda j, i: (0, j))],
        out_specs=pl.BlockSpec((tm, tn), lambda j, i: (i, j)),
        out_shape=jax.ShapeDtypeStruct((m, n), out_dtype),
        scratch_shapes=[pltpu.VMEM((k, tn), F32 if exact else BF16)],
        compiler_params=_params("arbitrary", "arbitrary"),
        name="matmul",
    )(x, w, b)


def _glu_body(x_ref, wa_ref, wg_ref, ba_ref, bg_ref, o_ref, wab_ref, wgb_ref):
    @pl.when(pl.program_id(1) == 0)
    def _():
        wab_ref[...] = wa_ref[...].astype(BF16)
        wgb_ref[...] = wg_ref[...].astype(BF16)

    x = x_ref[...].astype(BF16)
    a = jnp.dot(x, wab_ref[...], preferred_element_type=F32) + ba_ref[...]
    g = jnp.dot(x, wgb_ref[...], preferred_element_type=F32) + bg_ref[...]
    o_ref[...] = (a * jax.nn.sigmoid(g)).astype(o_ref.dtype)


def _glu_matmul(x, w, b, layer, tm=512, tn=512):
    m, k = x.shape
    n = w.shape[2] // 2
    tm = _tile(m, tm, SUBLANES)
    tn = _tile(n, tn, LANES)
    nj = n // tn
    b = b[layer].reshape(1, 2 * n)
    return pl.pallas_call(
        _glu_body,
        grid=(nj, m // tm),
        in_specs=[pl.BlockSpec((tm, k), lambda j, i: (i, 0)),
                  pl.BlockSpec((None, k, tn), lambda j, i: (layer, 0, j)),
                  pl.BlockSpec((None, k, tn), lambda j, i: (layer, 0, j + nj)),
                  pl.BlockSpec((1, tn), lambda j, i: (0, j)),
                  pl.BlockSpec((1, tn), lambda j, i: (0, j + nj))],
        out_specs=pl.BlockSpec((tm, tn), lambda j, i: (i, j)),
        out_shape=jax.ShapeDtypeStruct((m, n), F32),
        scratch_shapes=[pltpu.VMEM((k, tn), BF16), pltpu.VMEM((k, tn), BF16)],
        compiler_params=_params("arbitrary", "arbitrary"),
        name="glu_matmul",
    )(x, w, w, b, b)


def _rms_mod(x, g, shift, scale):
    y = x * lax.rsqrt(jnp.mean(x * x, axis=-1, keepdims=True) + RMS_EPS)
    return (y * g) * (1.0 + scale) + shift


def _normmod_body(x_ref, g_ref, sh_ref, sc_ref, h_ref):
    h_ref[0] = _rms_mod(x_ref[0], g_ref[...], sh_ref[0], sc_ref[0]).astype(h_ref.dtype)


def _resnormmod_body(x_ref, y_ref, gt_ref, g_ref, sh_ref, sc_ref, xo_ref, h_ref):
    x = x_ref[0] + gt_ref[0] * y_ref[0]
    xo_ref[0] = x
    h_ref[0] = _rms_mod(x, g_ref[...], sh_ref[0], sc_ref[0]).astype(h_ref.dtype)


def _resnorm_body(x_ref, y_ref, gt_ref, g_ref, o_ref):
    x = x_ref[0] + gt_ref[0] * y_ref[0]
    o_ref[0] = (x * lax.rsqrt(jnp.mean(x * x, axis=-1, keepdims=True) + RMS_EPS)) * g_ref[...]


def _mod_spec(mod, tt):
    if mod.shape[1] == 1:
        return pl.BlockSpec((1, 1, mod.shape[2]), lambda b, t: (b, 0, 0))
    return pl.BlockSpec((1, tt, mod.shape[2]), lambda b, t: (b, t, 0))


def _normmod(x, g, shift, scale, out_dtype):
    bsz, t, d = x.shape
    tt = _tile(t, 512, SUBLANES)
    row = pl.BlockSpec((1, tt, d), lambda b, i: (b, i, 0))
    return pl.pallas_call(
        _normmod_body,
        grid=(bsz, t // tt),
        in_specs=[row, pl.BlockSpec((1, d), lambda b, i: (0, 0)), _mod_spec(shift, tt), _mod_spec(scale, tt)],
        out_specs=row,
        out_shape=jax.ShapeDtypeStruct((bsz, t, d), out_dtype),
        compiler_params=_params("arbitrary", "arbitrary"),
        name="normmod",
    )(x, g.reshape(1, d), shift, scale)


def _resnormmod(x, y, gate, g, shift, scale, out_dtype):
    bsz, t, d = x.shape
    tt = _tile(t, 512, SUBLANES)
    row = pl.BlockSpec((1, tt, d), lambda b, i: (b, i, 0))
    return pl.pallas_call(
        _resnormmod_body,
        grid=(bsz, t // tt),
        in_specs=[row, row, _mod_spec(gate, tt), pl.BlockSpec((1, d), lambda b, i: (0, 0)),
                  _mod_spec(shift, tt), _mod_spec(scale, tt)],
        out_specs=[row, row],
        out_shape=[jax.ShapeDtypeStruct((bsz, t, d), F32), jax.ShapeDtypeStruct((bsz, t, d), out_dtype)],
        compiler_params=_params("arbitrary", "arbitrary"),
        name="resnormmod",
    )(x, y, gate, g.reshape(1, d), shift, scale)


def _resnorm(x, y, gate, g):
    bsz, t, d = x.shape
    tt = _tile(t, 512, SUBLANES)
    row = pl.BlockSpec((1, tt, d), lambda b, i: (b, i, 0))
    return pl.pallas_call(
        _resnorm_body,
        grid=(bsz, t // tt),
        in_specs=[row, row, _mod_spec(gate, tt), pl.BlockSpec((1, d), lambda b, i: (0, 0))],
        out_specs=row,
        out_shape=jax.ShapeDtypeStruct((bsz, t, d), F32),
        compiler_params=_params("arbitrary", "arbitrary"),
        name="resnorm",
    )(x, y, gate, g.reshape(1, d))


def _wkv_body(r_ref, lw_ref, k_ref, v_ref, a_ref, b_ref, s0_ref, y_ref, st_ref, s_scr, *, chunk, pairs, n_chunks,
              n_levels):
    tstep = pl.program_id(2)

    hs = HEAD_SIZE

    @pl.when(tstep == 0)
    def _():
        s_scr[...] = jnp.zeros_like(s_scr)
        for p in range(pairs):
            s_scr[p, 0:hs, 0:hs] = s0_ref[0, 2 * p]
            s_scr[p, hs:2 * hs, hs:2 * hs] = s0_ref[0, 2 * p + 1]

    rows = 2 * chunk
    lane = lax.broadcasted_iota(jnp.int32, (chunk, LANES), 1)
    head0 = (lane < HEAD_SIZE).astype(F32)
    head1 = 1.0 - head0
    ri = lax.broadcasted_iota(jnp.int32, (rows, rows), 0)
    ci = lax.broadcasted_iota(jnp.int32, (rows, rows), 1)
    strict = ri > ci
    incl = ri >= ci
    eye = (ri == ci).astype(F32)
    merge = [strict & ((ri >> (q + 1)) == (ci >> (q + 1))) & ((ri >> q) != (ci >> q)) for q in range(n_levels)]
    ti = lax.broadcasted_iota(jnp.int32, (chunk, chunk), 0)
    tj = lax.broadcasted_iota(jnp.int32, (chunk, chunk), 1)
    tri = (ti >= tj).astype(F32)

    def stack2(x):
        return jnp.concatenate([x * head0, x * head1], axis=0)

    def one_chunk(c, carry):
        sl = pl.ds(pl.multiple_of(c * chunk, chunk), chunk)
        ps = range(pairs)
        ls = [slice(p * LANES, (p + 1) * LANES) for p in ps]
        lw = [lw_ref[0, sl, ls[p]] for p in ps]
        cl = [jnp.dot(tri, lw[p], precision=HIGHEST, preferred_element_type=F32) for p in ps]
        c_in = [jnp.exp(cl[p]) for p in ps]
        c_inv = [jnp.exp(-cl[p]) for p in ps]
        c_ex = [jnp.exp(cl[p] - lw[p]) for p in ps]
        rt = [stack2(r_ref[0, sl, ls[p]] * c_in[p]).astype(BF16) for p in ps]
        at = [stack2(a_ref[0, sl, ls[p]] * c_ex[p]).astype(BF16) for p in ps]
        bt = [stack2(b_ref[0, sl, ls[p]] * c_inv[p]).astype(BF16) for p in ps]
        kt = [stack2(k_ref[0, sl, ls[p]] * c_inv[p]).astype(BF16) for p in ps]
        v2 = [stack2(v_ref[0, sl, ls[p]]).astype(BF16) for p in ps]
        s = [s_scr[p] for p in ps]
        sb = [s[p].astype(BF16) for p in ps]
        l_ab = [jnp.where(strict, _bdot(at[p], bt[p], _NT), 0.0) for p in ps]
        l_ak = [jnp.where(strict, _bdot(at[p], kt[p], _NT), 0.0).astype(BF16) for p in ps]
        m_rb = [jnp.where(incl, _bdot(rt[p], bt[p], _NT), 0.0).astype(BF16) for p in ps]
        m_rk = [jnp.where(incl, _bdot(rt[p], kt[p], _NT), 0.0).astype(BF16) for p in ps]
        rhs = [_bdot(at[p], sb[p], _NT) + _bdot(l_ak[p], v2[p]) for p in ps]
        y0 = [_bdot(rt[p], sb[p], _NT) + _bdot(m_rk[p], v2[p]) for p in ps]
        tinv = [eye + jnp.where(merge[0], l_ab[p], 0.0) for p in ps]
        for lv in range(1, n_levels):
            tb = [tinv[p].astype(BF16) for p in ps]
            off = [_bdot(jnp.where(merge[lv], l_ab[p], 0.0), tb[p]) for p in ps]
            tinv = [tinv[p] + _bdot(tb[p], off[p]) for p in ps]
        u = [_bdot(tinv[p], rhs[p]).astype(BF16) for p in ps]
        y2 = [y0[p] + _bdot(m_rb[p], u[p]) for p in ps]
        upd = [_bdot(u[p], bt[p], _TN) + _bdot(v2[p], kt[p], _TN) for p in ps]
        for p in ps:
            y_ref[0, sl, ls[p]] = y2[p][:chunk] + y2[p][chunk:]
            s_scr[p] = (s[p] + upd[p]) * c_in[p][chunk - 1:chunk, :]
        return carry

    lax.fori_loop(0, n_chunks, one_chunk, 0)

    @pl.when(tstep == pl.num_programs(2) - 1)
    def _():
        for p in range(pairs):
            st_ref[0, 2 * p] = s_scr[p, 0:hs, 0:hs]
            st_ref[0, 2 * p + 1] = s_scr[p, hs:2 * hs, hs:2 * hs]


def _wkv(r, lw, k, v, a, b, s0, layer):
    bsz, t, d = r.shape
    t_real = t
    if t % SUBLANES:
        pad = SUBLANES - t % SUBLANES
        r, lw, k, v, a, b = (jnp.pad(z, ((0, 0), (0, pad), (0, 0))) for z in (r, lw, k, v, a, b))
        t += pad
    chunk = _tile(t, WKV_CHUNK, SUBLANES)
    tt = _tile(t, WKV_TIME_TILE, chunk)
    n_pairs = d // LANES
    pairs = _tile(n_pairs, WKV_PAIRS, 1)
    n_levels = (chunk - 1).bit_length()
    assert chunk == 1 << n_levels, "the block-merge inverse needs a power-of-two chunk"
    seq = pl.BlockSpec((1, tt, pairs * LANES), lambda i, p, j: (i, j, p))
    st_shape = (2 * pairs, HEAD_SIZE, HEAD_SIZE)
    y, s_t = pl.pallas_call(
        functools.partial(_wkv_body, chunk=chunk, pairs=pairs, n_chunks=tt // chunk, n_levels=n_levels),
        grid=(bsz, n_pairs // pairs, t // tt),
        in_specs=[seq] * 6 + [pl.BlockSpec((None, 1) + st_shape, lambda i, p, j: (layer, i, p, 0, 0))],
        out_specs=[seq, pl.BlockSpec((1,) + st_shape, lambda i, p, j: (i, p, 0, 0))],
        out_shape=[jax.ShapeDtypeStruct((bsz, t, d), F32),
                   jax.ShapeDtypeStruct((bsz, 2 * n_pairs, HEAD_SIZE, HEAD_SIZE), F32)],
        scratch_shapes=[pltpu.VMEM((pairs, LANES, LANES), F32)],
        compiler_params=_params("arbitrary", "arbitrary", "arbitrary"),
        name="wkv",
    )(r, lw, k, v, a, b, s0)
    return y[:, :t_real], s_t


def _dwconv_body(u_ref, buf_ref, w_ref, b_ref, lnw_ref, lnb_ref, o_ref, win_ref, *, tt, width, halo):
    @pl.when(pl.program_id(1) == 0)
    def _():
        win_ref[0:halo, :] = buf_ref[0]

    win_ref[halo:halo + tt, :] = u_ref[0]
    first = halo - (width - 1)
    z = b_ref[...] + w_ref[0:1, :] * win_ref[first:first + tt, :]
    for j in range(1, width):
        z = z + w_ref[j:j + 1, :] * win_ref[first + j:first + j + tt, :]
    mu = jnp.mean(z, axis=-1, keepdims=True)
    zc = z - mu
    var = jnp.mean(zc * zc, axis=-1, keepdims=True)
    zn = zc * lax.rsqrt(var + CONV_LN_EPS) * lnw_ref[...] + lnb_ref[...]
    o_ref[0] = (zn * jax.nn.sigmoid(zn)).astype(o_ref.dtype)
    win_ref[0:halo, :] = win_ref[tt:tt + halo, :]


def _dwconv_ln_silu(u, buf, w_dw, b_dw, ln_w, ln_b):
    bsz, t, d = u.shape
    width = w_dw.shape[0]
    halo = -(-(width - 1) // SUBLANES) * SUBLANES
    tt = _tile(t, 128, SUBLANES)
    assert tt >= halo or tt == t
    bufp = jnp.pad(buf, ((0, 0), (halo - (width - 1), 0), (0, 0)))
    vec = pl.BlockSpec((1, d), lambda i, j: (0, 0))
    return pl.pallas_call(
        functools.partial(_dwconv_body, tt=tt, width=width, halo=halo),
        grid=(bsz, t // tt),
        in_specs=[pl.BlockSpec((1, tt, d), lambda i, j: (i, j, 0)),
                  pl.BlockSpec((1, halo, d), lambda i, j: (i, 0, 0)),
                  pl.BlockSpec((width, d), lambda i, j: (0, 0)), vec, vec, vec],
        out_specs=pl.BlockSpec((1, tt, d), lambda i, j: (i, j, 0)),
        out_shape=jax.ShapeDtypeStruct((bsz, t, d), BF16),
        scratch_shapes=[pltpu.VMEM((halo + max(tt, halo), d), F32)],
        compiler_params=_params("arbitrary", "arbitrary"),
        name="dwconv_ln_silu",
    )(u, bufp, w_dw, b_dw.reshape(1, d), ln_w.reshape(1, d), ln_b.reshape(1, d))


def _pool_body(h_ref, buf_ref, w_ref, sc_ref, o_ref, win_ref, wb_ref, *, tt, halo, start_pos, group):
    tstep = pl.program_id(1)

    @pl.when((pl.program_id(0) == 0) & (tstep == 0))
    def _():
        wb_ref[...] = w_ref[...].astype(BF16)

    @pl.when(tstep == 0)
    def _():
        win_ref[0:halo, :] = buf_ref[0]

    win_ref[halo:halo + tt, :] = h_ref[0]
    pos = start_pos + tstep * tt + lax.broadcasted_iota(jnp.int32, (tt, 1), 0)
    for gi, win in enumerate(POOL_WINDOWS):
        cs = slice(gi * group, (gi + 1) * group)
        tot = win_ref[halo:halo + tt, cs]
        for back in range(1, win):
            tot = tot + win_ref[halo - back:halo - back + tt, cs]
        cnt = jnp.minimum(pos + 1, win).astype(F32)
        dlt = tot / cnt - win_ref[halo:halo + tt, cs]
        o_ref[0, :, cs] = jnp.dot(dlt.astype(BF16), wb_ref[gi], preferred_element_type=F32) * sc_ref[:, cs]
    win_ref[0:halo, :] = win_ref[tt:tt + halo, :]


def _pool_mixer(h, buf, start_pos, w_grp, scale):
    bsz, t, d = h.shape
    ng, group, _ = w_grp.shape
    halo = -(-POOL_BUF // SUBLANES) * SUBLANES
    tt = _tile(t, 256, SUBLANES)
    assert tt >= halo or tt == t
    bufp = jnp.pad(buf, ((0, 0), (halo - POOL_BUF, 0), (0, 0)))
    return pl.pallas_call(
        functools.partial(_pool_body, tt=tt, halo=halo, start_pos=start_pos, group=group),
        grid=(bsz, t // tt),
        in_specs=[pl.BlockSpec((1, tt, d), lambda i, j: (i, j, 0)),
                  pl.BlockSpec((1, halo, d), lambda i, j: (i, 0, 0)),
                  pl.BlockSpec((ng, group, group), lambda i, j: (0, 0, 0)),
                  pl.BlockSpec((1, d), lambda i, j: (0, 0))],
        out_specs=pl.BlockSpec((1, tt, d), lambda i, j: (i, j, 0)),
        out_shape=jax.ShapeDtypeStruct((bsz, t, d), F32),
        scratch_shapes=[pltpu.VMEM((halo + max(tt, halo), d), F32), pltpu.VMEM((ng, group, group), BF16)],
        compiler_params=_params("arbitrary", "arbitrary"),
        name="pool_mixer",
    )(h, bufp, w_grp, scale.reshape(1, d))


def _stream_run_weights(meta_ref, cnt_ref, start_copy, stage, sem, cast):
    j, i = pl.program_id(0), pl.program_id(1)
    n_used, n_runs = cnt_ref[0], cnt_ref[1]

    @pl.when((i < n_used) & (meta_ref[1, i] == 1))
    def _():
        run = meta_ref[2, i]
        g = j * n_runs + run
        slot = lax.rem(g, 2)

        @pl.when(g == 0)
        def _():
            start_copy(meta_ref[0, 0], 0, 0)

        pltpu.make_async_copy(stage.at[slot], stage.at[slot], sem.at[slot]).wait()
        last_run = run + 1 == n_runs

        @pl.when(jnp.logical_not(last_run) | (j + 1 < pl.num_programs(0)))
        def _():
            start_copy(meta_ref[3, i], jnp.where(last_run, j + 1, j), 1 - slot)

        cast(slot)


def _moe_up_body(meta_ref, cnt_ref, x_ref, w_hbm, bg_ref, bu_ref, h_ref, stage, wb_ref, sem, *, layer, tf, f):
    def start_copy(e, tile, slot):
        col = pl.multiple_of(tile * tf, tf)
        w_e = w_hbm.at[layer, e]
        pltpu.make_async_copy(w_e.at[:, pl.ds(col, tf)], stage.at[slot, 0], sem.at[slot]).start()
        pltpu.make_async_copy(w_e.at[:, pl.ds(f + col, tf)], stage.at[slot, 1], sem.at[slot]).start()

    def cast(slot):
        wb_ref[0] = stage[slot, 0].astype(BF16)
        wb_ref[1] = stage[slot, 1].astype(BF16)

    _stream_run_weights(meta_ref, cnt_ref, start_copy, stage, sem, cast)
    i = pl.program_id(1)

    @pl.when(i < cnt_ref[0])
    def _():
        x = x_ref[...]
        g = jnp.dot(x, wb_ref[0], preferred_element_type=F32) + bg_ref[...]
        u = jnp.dot(x, wb_ref[1], preferred_element_type=F32) + bu_ref[...]
        g = jnp.minimum(g, SWIGLU_LIMIT)
        u = jnp.clip(u, -SWIGLU_LIMIT, SWIGLU_LIMIT)
        h_ref[...] = ((u + 1.0) * g * jax.nn.sigmoid(SWIGLU_ALPHA * g)).astype(h_ref.dtype)

    @pl.when(i >= cnt_ref[0])
    def _():
        h_ref[...] = jnp.zeros_like(h_ref)


def _moe_down_body(meta_ref, cnt_ref, h_ref, w_hbm, b_ref, gate_ref, y_ref, stage, wb_ref, sem, *, layer, tn):
    def start_copy(e, tile, slot):
        col = pl.multiple_of(tile * tn, tn)
        pltpu.make_async_copy(w_hbm.at[layer, e].at[:, pl.ds(col, tn)], stage.at[slot], sem.at[slot]).start()

    def cast(slot):
        wb_ref[...] = stage[slot].astype(BF16)

    _stream_run_weights(meta_ref, cnt_ref, start_copy, stage, sem, cast)
    i = pl.program_id(1)

    @pl.when(i < cnt_ref[0])
    def _():
        y = jnp.dot(h_ref[...], wb_ref[...], preferred_element_type=F32) + b_ref[...]
        y_ref[...] = y * gate_ref[...]

    @pl.when(i >= cnt_ref[0])
    def _():
        y_ref[...] = jnp.zeros_like(y_ref)


def _moe_experts(xs, blk_e, n_used, row_gate, w_gu, b_gu, w_down, b_down, layer):
    n_rows, d = xs.shape
    _, n_exp, _, f2 = w_gu.shape
    f = f2 // 2
    bm = MOE_ROWS
    n_blocks = n_rows // bm
    tf = _tile(f, 512, LANES)
    nf = f // tf
    blk = jnp.arange(n_blocks, dtype=jnp.int32)
    first = ((blk == 0) | (blk_e != jnp.roll(blk_e, 1))) & (blk < n_used[0])
    run = jnp.cumsum(first.astype(jnp.int32)) - 1
    starts = jnp.where(first, blk, n_blocks)
    later = jnp.concatenate([lax.cummin(starts, reverse=True)[1:], jnp.full((1,), n_blocks, jnp.int32)])
    nxt_e = blk_e[jnp.where(later < n_blocks, later, 0)]
    meta = jnp.stack([blk_e, first.astype(jnp.int32), run, nxt_e]).astype(jnp.int32)
    cnt = jnp.concatenate([n_used, jnp.sum(first.astype(jnp.int32)).reshape(1)]).astype(jnp.int32)
    b_gu3 = b_gu[layer].reshape(n_exp, 1, f2)
    hidden = pl.pallas_call(
        functools.partial(_moe_up_body, layer=layer, tf=tf, f=f),
        grid_spec=pltpu.PrefetchScalarGridSpec(
            num_scalar_prefetch=2,
            grid=(nf, n_blocks),
            in_specs=[pl.BlockSpec((bm, d), lambda j, i, mt, ct: (i, 0)),
                      pl.BlockSpec(memory_space=pl.ANY),
                      pl.BlockSpec((None, 1, tf), lambda j, i, mt, ct: (mt[0, i], 0, j)),
                      pl.BlockSpec((None, 1, tf), lambda j, i, mt, ct: (mt[0, i], 0, j + nf))],
            out_specs=pl.BlockSpec((bm, tf), lambda j, i, mt, ct: (i, j)),
            scratch_shapes=[pltpu.VMEM((2, 2, d, tf), F32), pltpu.VMEM((2, d, tf), BF16),
                            pltpu.SemaphoreType.DMA((2,))]),
        out_shape=jax.ShapeDtypeStruct((n_rows, f), BF16),
        compiler_params=_params("arbitrary", "arbitrary"),
        name="moe_up",
    )(meta, cnt, xs, w_gu, b_gu3, b_gu3)
    tn = _tile(d, 512, LANES)
    return pl.pallas_call(
        functools.partial(_moe_down_body, layer=layer, tn=tn),
        grid_spec=pltpu.PrefetchScalarGridSpec(
            num_scalar_prefetch=2,
            grid=(d // tn, n_blocks),
            in_specs=[pl.BlockSpec((bm, f), lambda j, i, mt, ct: (i, 0)),
                      pl.BlockSpec(memory_space=pl.ANY),
                      pl.BlockSpec((None, 1, tn), lambda j, i, mt, ct: (mt[0, i], 0, j)),
                      pl.BlockSpec((bm, 1), lambda j, i, mt, ct: (i, 0))],
            out_specs=pl.BlockSpec((bm, tn), lambda j, i, mt, ct: (i, j)),
            scratch_shapes=[pltpu.VMEM((2, f, tn), F32), pltpu.VMEM((f, tn), BF16),
                            pltpu.SemaphoreType.DMA((2,))]),
        out_shape=jax.ShapeDtypeStruct((n_rows, d), F32),
        compiler_params=_params("arbitrary", "arbitrary"),
        name="moe_down",
    )(meta, cnt, hidden, w_down, b_down[layer].reshape(n_exp, 1, d), row_gate.reshape(n_rows, 1))


def _moe(hf, w_router, b_router, w_gu, b_gu, w_down, b_down, layer):
    n, d = hf.shape
    n_exp = w_router.shape[2]
    bm = MOE_ROWS
    logits = _matmul(hf, w_router, b_router, layer, exact=True)
    top_v, top_e = lax.top_k(logits, TOP_K)
    gates = jax.nn.softmax(top_v, axis=-1).reshape(-1)
    n_assign = n * TOP_K
    flat_e = top_e.reshape(n_assign).astype(jnp.int32)
    onehot = (flat_e[:, None] == jnp.arange(n_exp, dtype=jnp.int32)[None, :]).astype(jnp.int32)
    running = jnp.cumsum(onehot, axis=0)
    counts = running[-1]
    rank = jnp.take_along_axis(running, flat_e[:, None], axis=1)[:, 0] - 1
    padded = (counts + bm - 1) // bm * bm
    pend = jnp.cumsum(padded)
    pstart = pend - padded
    start = jnp.cumsum(counts) - counts
    pos = pstart[flat_e] + rank
    n_blocks = -(-(n_assign + n_exp * (bm - 1)) // bm)
    n_rows = n_blocks * bm
    blk_e = jnp.minimum(jnp.searchsorted(pend, jnp.arange(n_blocks, dtype=jnp.int32) * bm, side="right"),
                        n_exp - 1).astype(jnp.int32)
    n_used = (pend[-1] // bm).astype(jnp.int32).reshape(1)
    order = jnp.argsort(flat_e, stable=True).astype(jnp.int32)
    row = jnp.arange(n_rows, dtype=jnp.int32)
    row_e = blk_e[row // bm]
    row_rank = row - pstart[row_e]
    valid = row_rank < counts[row_e]
    src = order[jnp.clip(start[row_e] + row_rank, 0, n_assign - 1)]
    row_tok = jnp.where(valid, src // TOP_K, 0)
    row_gate = jnp.where(valid, gates[src], 0.0)
    xs = hf.astype(BF16)[row_tok]
    y = _moe_experts(xs, blk_e, n_used, row_gate, w_gu, b_gu, w_down, b_down, layer)
    return y, pos.reshape(n, TOP_K)


def _combine_body(pos_ref, nxt_ref, y_hbm, x_ref, gt_ref, g_ref, *rest, tm, final):
    if final:
        o_ref, buf, sem = rest
    else:
        sh_ref, sc_ref, xo_ref, h_ref, buf, sem = rest
    step = pl.program_id(0) * pl.num_programs(1) + pl.program_id(1)
    n_steps = pl.num_programs(0) * pl.num_programs(1)
    slot = lax.rem(step, 2)

    def start_block(idx_ref, dst_slot):
        def body(r, carry):
            for k in range(TOP_K):
                row = idx_ref[0, r * TOP_K + k]
                pltpu.make_async_copy(y_hbm.at[pl.ds(row, 1)], buf.at[dst_slot, k, pl.ds(r, 1)],
                                      sem.at[dst_slot]).start()
            return carry
        lax.fori_loop(0, tm, body, 0, unroll=8)

    @pl.when(step == 0)
    def _():
        start_block(pos_ref, 0)

    @pl.when(step + 1 < n_steps)
    def _():
        start_block(nxt_ref, 1 - slot)

    pltpu.make_async_copy(buf.at[slot], buf.at[slot], sem.at[slot]).wait()
    moe = buf[slot, 0]
    for k in range(1, TOP_K):
        moe = moe + buf[slot, k]
    x = x_ref[0] + gt_ref[0] * moe
    if final:
        o_ref[0] = (x * lax.rsqrt(jnp.mean(x * x, axis=-1, keepdims=True) + RMS_EPS)) * g_ref[...]
    else:
        xo_ref[0] = x
        h_ref[0] = _rms_mod(x, g_ref[...], sh_ref[0], sc_ref[0]).astype(h_ref.dtype)


def _moe_combine(y, pos, x, gate, g, shift=None, scale=None):
    bsz, t, d = x.shape
    final = shift is None
    tm = _tile(t, 128, SUBLANES)
    nt = t // tm
    n_steps = bsz * nt
    pos3 = pos.astype(jnp.int32).reshape(n_steps, 1, tm * TOP_K)
    idx = pl.BlockSpec((None, 1, tm * TOP_K), lambda i, j: (i * nt + j, 0, 0), memory_space=pltpu.SMEM)
    nxt = pl.BlockSpec((None, 1, tm * TOP_K), lambda i, j: (jnp.minimum(i * nt + j + 1, n_steps - 1), 0, 0),
                       memory_space=pltpu.SMEM)
    row = pl.BlockSpec((1, tm, d), lambda i, j: (i, j, 0))
    vec = pl.BlockSpec((1, d), lambda i, j: (0, 0))
    in_specs = [idx, nxt, pl.BlockSpec(memory_space=pl.ANY), row, _mod_spec(gate, tm), vec]
    args = [pos3, pos3, y, x, gate, g.reshape(1, d)]
    if final:
        out_specs, out_shape = row, jax.ShapeDtypeStruct((bsz, t, d), F32)
    else:
        in_specs += [_mod_spec(shift, tm), _mod_spec(scale, tm)]
        args += [shift, scale]
        out_specs = [row, row]
        out_shape = [jax.ShapeDtypeStruct((bsz, t, d), F32), jax.ShapeDtypeStruct((bsz, t, d), F32)]
    return pl.pallas_call(
        functools.partial(_combine_body, tm=tm, final=final),
        grid=(bsz, nt),
        in_specs=in_specs,
        out_specs=out_specs,
        out_shape=out_shape,
        scratch_shapes=[pltpu.VMEM((2, TOP_K, tm, d), F32), pltpu.SemaphoreType.DMA((2,))],
        compiler_params=_params("arbitrary", "arbitrary"),
        name="moe_combine",
    )(*args)


def _rwkv7(h, shift_prev, s0, v_first, p, j):
    bsz, t, d = h.shape
    n = bsz * t
    heads = d // HEAD_SIZE
    prev = jnp.concatenate([shift_prev[:, None], h[:, :-1]], axis=1)
    xx = prev - h
    mix = p["rwkv_mix"][j]
    xr, xw, xk, xv, xa, xg = ((h + xx * mix[m]).astype(BF16).reshape(n, d) for m in range(6))
    r = _matmul(xr, p["rwkv_wr"], layer=j)
    k = _matmul(xk, p["rwkv_wk"], layer=j)
    v = _matmul(xv, p["rwkv_wv"], layer=j)
    if j == 0:
        v_first = v
    else:
        v_lora = _matmul(_matmul(xv, p["rwkv_v1"], layer=j - 1, out_dtype=BF16), p["rwkv_v2"], layer=j - 1)
        v = v + (v_first - v) * jax.nn.sigmoid(p["rwkv_v0"][j - 1] + v_lora)
    w_lora = _matmul(_matmul(xw, p["rwkv_w1"], layer=j, act="tanh", out_dtype=BF16), p["rwkv_w2"], layer=j)
    logw = -jax.nn.softplus(-(p["rwkv_w0"][j] + w_lora)) - 0.5
    log_decay = -jnp.exp(logw)
    a_lora = _matmul(_matmul(xa, p["rwkv_a1"], layer=j, out_dtype=BF16), p["rwkv_a2"], layer=j)
    a = jax.nn.sigmoid(p["rwkv_a0"][j] + a_lora)
    g = _matmul(_matmul(xg, p["rwkv_g1"], layer=j, act="sigmoid", out_dtype=BF16), p["rwkv_g2"], layer=j)
    kk = (k * p["rwkv_kk"][j]).reshape(n, heads, HEAD_SIZE)
    kk = kk / jnp.maximum(jnp.sqrt(jnp.sum(kk * kk, axis=-1, keepdims=True)), 1e-12)
    kk = kk.reshape(n, d)
    k = k * (1.0 + (a - 1.0) * p["rwkv_ka"][j])
    seq = lambda z: z.reshape(bsz, t, d)
    y, s_t = _wkv(seq(r), seq(log_decay), seq(k), seq(v), seq(-kk), seq(kk * a), s0, j)
    y = y.reshape(n, heads, HEAD_SIZE)
    mu = jnp.mean(y, axis=-1, keepdims=True)
    var = jnp.mean(jnp.square(y - mu), axis=-1, keepdims=True)
    y = ((y - mu) * lax.rsqrt(var + LN_X_EPS)).reshape(n, d) * p["rwkv_lnx_w"][j] + p["rwkv_lnx_b"][j]
    rk = (r * k).reshape(n, heads, HEAD_SIZE) * p["rwkv_rk"][j]
    bonus = (jnp.sum(rk, axis=-1, keepdims=True) * v.reshape(n, heads, HEAD_SIZE)).reshape(n, d)
    out = _matmul(((y + bonus) * g).astype(BF16), p["rwkv_wo"], layer=j)
    return out.reshape(bsz, t, d), h[:, -1], s_t, v_first


def _conformer_conv(h, buf, p, j):
    bsz, t, d = h.shape
    width = p["conv_w_dw"].shape[1]
    u = _glu_matmul(h.reshape(bsz * t, d), p["conv_w_in"], p["conv_b_in"], j).reshape(bsz, t, -1)
    z = _dwconv_ln_silu(u, buf, p["conv_w_dw"][j], p["conv_b_dw"][j], p["conv_ln_w"][j], p["conv_ln_b"][j])
    out = _matmul(z.reshape(bsz * t, -1), p["conv_w_out"], p["conv_b_out"], j).reshape(bsz, t, d)
    new_buf = jnp.concatenate([buf, u], axis=1)[:, -(width - 1):]
    return out, new_buf


def _pool(h, buf, start_pos, p, j):
    out = _pool_mixer(h, buf, start_pos, p["pool_w"][j], p["pool_scale"][j])
    return out, jnp.concatenate([buf, h], axis=1)[:, -POOL_BUF:]


def kernel(x_prompt, x_sample, c_prompt, c_sample, state_wkv, state_shift, state_conv, state_pool, ada_w, ada_b, norm_mix, norm_ffn, final_norm, rwkv_mix, rwkv_wr, rwkv_wk, rwkv_wv, rwkv_wo, rwkv_w0, rwkv_w1, rwkv_w2, rwkv_a0, rwkv_a1, rwkv_a2, rwkv_v0, rwkv_v1, rwkv_v2, rwkv_g1, rwkv_g2, rwkv_kk, rwkv_ka, rwkv_rk, rwkv_lnx_w, rwkv_lnx_b, conv_w_in, conv_b_in, conv_w_dw, conv_b_dw, conv_ln_w, conv_ln_b, conv_w_out, conv_b_out, pool_w, pool_scale, moe_w_router, moe_b_router, moe_w_gu, moe_b_gu, moe_w_down, moe_b_down):
    p = dict(rwkv_mix=rwkv_mix, rwkv_wr=rwkv_wr, rwkv_wk=rwkv_wk, rwkv_wv=rwkv_wv, rwkv_wo=rwkv_wo,
             rwkv_w0=rwkv_w0, rwkv_w1=rwkv_w1, rwkv_w2=rwkv_w2, rwkv_a0=rwkv_a0, rwkv_a1=rwkv_a1,
             rwkv_a2=rwkv_a2, rwkv_v0=rwkv_v0, rwkv_v1=rwkv_v1, rwkv_v2=rwkv_v2, rwkv_g1=rwkv_g1,
             rwkv_g2=rwkv_g2, rwkv_kk=rwkv_kk, rwkv_ka=rwkv_ka, rwkv_rk=rwkv_rk, rwkv_lnx_w=rwkv_lnx_w,
             rwkv_lnx_b=rwkv_lnx_b, conv_w_in=conv_w_in, conv_b_in=conv_b_in, conv_w_dw=conv_w_dw,
             conv_b_dw=conv_b_dw, conv_ln_w=conv_ln_w, conv_ln_b=conv_ln_b, conv_w_out=conv_w_out,
             conv_b_out=conv_b_out, pool_w=pool_w, pool_scale=pool_scale)
    depth = ada_w.shape[0]
    bp, tp, d = x_prompt.shape
    bs, ts, _ = x_sample.shape
    heads = d // HEAD_SIZE
    n_p, n_s = bp * tp, bs * ts
    width = conv_w_dw.shape[1]

    c_act = jax.nn.silu(jnp.concatenate([c_prompt, c_sample], axis=0))
    mods = []
    for i in range(depth):
        m = _matmul(c_act, ada_w, ada_b, i)
        six = jnp.split(m, 6, axis=-1)
        mods.append(([z[:bp, None, :] for z in six],
                     [jnp.repeat(z[bp:], ts, axis=0)[None] for z in six]))

    groups = [
        dict(x=x_prompt, b=bp, t=tp, start=0,
             wkv=jnp.zeros((state_wkv.shape[0], bp, heads, HEAD_SIZE, HEAD_SIZE), F32),
             shift=jnp.zeros((state_shift.shape[0], bp, d), F32),
             conv=jnp.zeros((state_conv.shape[0], bp, width - 1, conv_w_dw.shape[2]), F32),
             pool=jnp.zeros((state_pool.shape[0], bp, POOL_BUF, d), F32)),
        dict(x=x_sample.reshape(1, n_s, d), b=bs, t=ts, start=PAST_LEN,
             wkv=state_wkv, shift=state_shift, conv=state_conv, pool=state_pool),
    ]
    for gi, grp in enumerate(groups):
        grp.update(wkv_out=[], shift_out=[], conv_out=[], pool_out=[], v_first=None)
        sh1, sc1 = mods[0][gi][0], mods[0][gi][1]
        grp["h"] = _normmod(grp["x"], norm_mix[0], sh1, sc1, F32)

    for i in range(depth):
        kind = i % 3
        hfs = []
        for gi, grp in enumerate(groups):
            sh1, sc1, gt1, sh2, sc2, gt2 = mods[i][gi]
            h = grp["h"].reshape(grp["b"], grp["t"], d)
            if kind == 0:
                j = len(grp["wkv_out"])
                out, last, s_t, grp["v_first"] = _rwkv7(h, grp["shift"][j], grp["wkv"], grp["v_first"], p, j)
                grp["wkv_out"].append(s_t)
                grp["shift_out"].append(last)
            elif kind == 1:
                j = len(grp["conv_out"])
                out, buf = _conformer_conv(h, grp["conv"][j], p, j)
                grp["conv_out"].append(buf)
            else:
                j = len(grp["pool_out"])
                out, buf = _pool(h, grp["pool"][j], grp["start"], p, j)
                grp["pool_out"].append(buf)
            grp["x"], hf = _resnormmod(grp["x"], out.reshape(grp["x"].shape), gt1, norm_ffn[i], sh2, sc2, F32)
            hfs.append(hf.reshape(-1, d))
        y, pos = _moe(jnp.concatenate(hfs, axis=0), moe_w_router, moe_b_router, moe_w_gu, moe_b_gu, moe_w_down,
                      moe_b_down, i)
        for gi, grp in enumerate(groups):
            gt2 = mods[i][gi][5]
            pos_g = pos[:n_p] if gi == 0 else pos[n_p:]
            if i + 1 < depth:
                sh1, sc1 = mods[i + 1][gi][0], mods[i + 1][gi][1]
                grp["x"], grp["h"] = _moe_combine(y, pos_g, grp["x"], gt2, norm_mix[i + 1], sh1, sc1)
            else:
                grp["x"] = _moe_combine(y, pos_g, grp["x"], gt2, final_norm)

    gp, gs = groups
    return (gp["x"], gs["x"].reshape(bs, ts, d),
            jnp.stack(gp["wkv_out"]), jnp.stack(gp["shift_out"]), jnp.stack(gp["conv_out"]), jnp.stack(gp["pool_out"]),
            jnp.stack(gs["wkv_out"]), jnp.stack(gs["shift_out"]), jnp.stack(gs["conv_out"]), jnp.stack(gs["pool_out"]))
```

```python
import functools

import jax
import jax.numpy as jnp
from jax import lax
from jax.experimental import pallas as pl
from jax.experimental.pallas import tpu as pltpu

F32 = jnp.float32
BF16 = jnp.bfloat16
HIGHEST = lax.Precision.HIGHEST

HEAD_SIZE = 64
TOP_K = 4
POOL_WINDOWS = (2, 4, 8, 16)
POOL_BUF = max(POOL_WINDOWS) - 1
PAST_LEN = 16384
RMS_EPS = 1e-5
LN_X_EPS = 64e-5
CONV_LN_EPS = 1e-5
SWIGLU_LIMIT = 7.0
SWIGLU_ALPHA = 1.702

VMEM_LIMIT_BYTES = 56 * 1024 * 1024
SUBLANES = 8
LANES = 128
WKV_CHUNK = 64
WKV_PAIRS = 16
WKV_TIME_TILE = 128
MOE_ROWS = 256
MOE_COLS = 1024
WEIGHT_DMA_PRIORITY = 1


def _tile(n, pref, mult):
    if n <= pref:
        return n
    t = pref - pref % mult
    while t >= mult:
        if n % t == 0:
            return t
        t -= mult
    return n


def _params(*sem):
    return pltpu.CompilerParams(dimension_semantics=sem, vmem_limit_bytes=VMEM_LIMIT_BYTES)


def _bdot(a, b, dims=(((1,), (0,)), ((), ()))):
    return lax.dot_general(a.astype(BF16), b.astype(BF16), dims, preferred_element_type=F32)


_NT = (((1,), (1,)), ((), ()))
_TN = (((0,), (0,)), ((), ()))


def _apply_act(acc, act):
    if act == "tanh":
        return jnp.tanh(acc)
    if act == "sigmoid":
        return jax.nn.sigmoid(acc)
    assert act is None, act
    return acc


def _mm_body(x_ref, w_ref, b_ref, o_ref, wb_ref, *, act, exact):
    @pl.when(pl.program_id(1) == 0)
    def _():
        wb_ref[...] = w_ref[...].astype(wb_ref.dtype)

    if exact:
        acc = jnp.dot(x_ref[...].astype(F32), wb_ref[...], precision=HIGHEST, preferred_element_type=F32)
    else:
        acc = jnp.dot(x_ref[...].astype(BF16), wb_ref[...], preferred_element_type=F32)
    o_ref[...] = _apply_act(acc + b_ref[...], act).astype(o_ref.dtype)


def _matmul(x, w, b=None, layer=0, act=None, out_dtype=F32, exact=False, tm=512, tn=1024):
    m, k = x.shape
    n = w.shape[2]
    tm = _tile(m, tm, SUBLANES)
    tn = _tile(n, tn, LANES)
    b = jnp.zeros((1, n), F32) if b is None else b[layer].reshape(1, n)
    return pl.pallas_call(
        functools.partial(_mm_body, act=act, exact=exact),
        grid=(n // tn, m // tm),
        in_specs=[pl.BlockSpec((tm, k), lambda j, i: (i, 0)),
                  pl.BlockSpec((None, k, tn), lambda j, i: (layer, 0, j)),
                  pl.BlockSpec((1, tn), lambda j, i: (0, j))],
        out_specs=pl.BlockSpec((tm, tn), lambda j, i: (i, j)),
        out_shape=jax.ShapeDtypeStruct((m, n), out_dtype),
        scratch_shapes=[pltpu.VMEM((k, tn), F32 if exact else BF16)],
        compiler_params=_params("arbitrary", "arbitrary"),
        name="matmul",
    )(x, w, b)


def _glu_body(x_ref, wa_ref, wg_ref, ba_ref, bg_ref, o_ref, wab_ref, wgb_ref):
    @pl.when(pl.program_id(1) == 0)
    def _():
        wab_ref[...] = wa_ref[...].astype(BF16)
        wgb_ref[...] = wg_ref[...].astype(BF16)

    x = x_ref[...].astype(BF16)
    a = jnp.dot(x, wab_ref[...], preferred_element_type=F32) + ba_ref[...]
    g = jnp.dot(x, wgb_ref[...], preferred_element_type=F32) + bg_ref[...]
    o_ref[...] = (a * jax.nn.sigmoid(g)).astype(o_ref.dtype)


def _glu_matmul(x, w, b, layer, tm=512, tn=512):
    m, k = x.shape
    n = w.shape[2] // 2
    tm = _tile(m, tm, SUBLANES)
    tn = _tile(n, tn, LANES)
    nj = n // tn
    b = b[layer].reshape(1, 2 * n)
    return pl.pallas_call(
        _glu_body,
        grid=(nj, m // tm),
        in_specs=[pl.BlockSpec((tm, k), lambda j, i: (i, 0)),
                  pl.BlockSpec((None, k, tn), lambda j, i: (layer, 0, j)),
                  pl.BlockSpec((None, k, tn), lambda j, i: (layer, 0, j + nj)),
                  pl.BlockSpec((1, tn), lambda j, i: (0, j)),
                  pl.BlockSpec((1, tn), lambda j, i: (0, j + nj))],
        out_specs=pl.BlockSpec((tm, tn), lambda j, i: (i, j)),
        out_shape=jax.ShapeDtypeStruct((m, n), F32),
        scratch_shapes=[pltpu.VMEM((k, tn), BF16), pltpu.VMEM((k, tn), BF16)],
        compiler_params=_params("arbitrary", "arbitrary"),
        name="glu_matmul",
    )(x, w, w, b, b)


def _rms_mod(x, g, shift, scale):
    y = x * lax.rsqrt(jnp.mean(x * x, axis=-1, keepdims=True) + RMS_EPS)
    return (y * g) * (1.0 + scale) + shift


def _normmod_body(x_ref, g_ref, sh_ref, sc_ref, h_ref):
    h_ref[0] = _rms_mod(x_ref[0], g_ref[...], sh_ref[0], sc_ref[0]).astype(h_ref.dtype)


def _resnormmod_body(x_ref, y_ref, gt_ref, g_ref, sh_ref, sc_ref, xo_ref, h_ref):
    x = x_ref[0] + gt_ref[0] * y_ref[0]
    xo_ref[0] = x
    h_ref[0] = _rms_mod(x, g_ref[...], sh_ref[0], sc_ref[0]).astype(h_ref.dtype)


def _resnorm_body(x_ref, y_ref, gt_ref, g_ref, o_ref):
    x = x_ref[0] + gt_ref[0] * y_ref[0]
    o_ref[0] = (x * lax.rsqrt(jnp.mean(x * x, axis=-1, keepdims=True) + RMS_EPS)) * g_ref[...]


def _mod_spec(mod, tt):
    if mod.shape[1] == 1:
        return pl.BlockSpec((1, 1, mod.shape[2]), lambda b, t: (b, 0, 0))
    return pl.BlockSpec((1, tt, mod.shape[2]), lambda b, t: (b, t, 0))


def _normmod(x, g, shift, scale, out_dtype):
    bsz, t, d = x.shape
    tt = _tile(t, 512, SUBLANES)
    row = pl.BlockSpec((1, tt, d), lambda b, i: (b, i, 0))
    return pl.pallas_call(
        _normmod_body,
        grid=(bsz, t // tt),
        in_specs=[row, pl.BlockSpec((1, d), lambda b, i: (0, 0)), _mod_spec(shift, tt), _mod_spec(scale, tt)],
        out_specs=row,
        out_shape=jax.ShapeDtypeStruct((bsz, t, d), out_dtype),
        compiler_params=_params("arbitrary", "arbitrary"),
        name="normmod",
    )(x, g.reshape(1, d), shift, scale)


def _resnormmod(x, y, gate, g, shift, scale, out_dtype):
    bsz, t, d = x.shape
    tt = _tile(t, 512, SUBLANES)
    row = pl.BlockSpec((1, tt, d), lambda b, i: (b, i, 0))
    return pl.pallas_call(
        _resnormmod_body,
        grid=(bsz, t // tt),
        in_specs=[row, row, _mod_spec(gate, tt), pl.BlockSpec((1, d), lambda b, i: (0, 0)),
                  _mod_spec(shift, tt), _mod_spec(scale, tt)],
        out_specs=[row, row],
        out_shape=[jax.ShapeDtypeStruct((bsz, t, d), F32), jax.ShapeDtypeStruct((bsz, t, d), out_dtype)],
        compiler_params=_params("arbitrary", "arbitrary"),
        name="resnormmod",
    )(x, y, gate, g.reshape(1, d), shift, scale)


def _resnorm(x, y, gate, g):
    bsz, t, d = x.shape
    tt = _tile(t, 512, SUBLANES)
    row = pl.BlockSpec((1, tt, d), lambda b, i: (b, i, 0))
    return pl.pallas_call(
        _resnorm_body,
        grid=(bsz, t // tt),
        in_specs=[row, row, _mod_spec(gate, tt), pl.BlockSpec((1, d), lambda b, i: (0, 0))],
        out_specs=row,
        out_shape=jax.ShapeDtypeStruct((bsz, t, d), F32),
        compiler_params=_params("arbitrary", "arbitrary"),
        name="resnorm",
    )(x, y, gate, g.reshape(1, d))


def _wkv_body(r_ref, lw_ref, k_ref, v_ref, a_ref, b_ref, s0_ref, y_ref, st_ref, s_scr, *, chunk, pairs, n_chunks,
              n_levels):
    tstep = pl.program_id(2)

    hs = HEAD_SIZE

    @pl.when(tstep == 0)
    def _():
        s_scr[...] = jnp.zeros_like(s_scr)
        for p in range(pairs):
            s_scr[p, 0:hs, 0:hs] = s0_ref[0, 2 * p]
            s_scr[p, hs:2 * hs, hs:2 * hs] = s0_ref[0, 2 * p + 1]

    rows = 2 * chunk
    lane = lax.broadcasted_iota(jnp.int32, (chunk, LANES), 1)
    head0 = (lane < HEAD_SIZE).astype(F32)
    head1 = 1.0 - head0
    ri = lax.broadcasted_iota(jnp.int32, (rows, rows), 0)
    ci = lax.broadcasted_iota(jnp.int32, (rows, rows), 1)
    strict = ri > ci
    incl = ri >= ci
    eye = (ri == ci).astype(F32)
    merge = [strict & ((ri >> (q + 1)) == (ci >> (q + 1))) & ((ri >> q) != (ci >> q)) for q in range(n_levels)]
    ti = lax.broadcasted_iota(jnp.int32, (chunk, chunk), 0)
    tj = lax.broadcasted_iota(jnp.int32, (chunk, chunk), 1)
    tri = (ti >= tj).astype(F32)

    def stack2(x):
        return jnp.concatenate([x * head0, x * head1], axis=0)

    def one_chunk(c, carry):
        sl = pl.ds(pl.multiple_of(c * chunk, chunk), chunk)
        ps = range(pairs)
        ls = [slice(p * LANES, (p + 1) * LANES) for p in ps]
        lw = [lw_ref[0, sl, ls[p]] for p in ps]
        cl = [jnp.dot(tri, lw[p], precision=HIGHEST, preferred_element_type=F32) for p in ps]
        c_in = [jnp.exp(cl[p]) for p in ps]
        c_inv = [jnp.exp(-cl[p]) for p in ps]
        c_ex = [jnp.exp(cl[p] - lw[p]) for p in ps]
        rt = [stack2(r_ref[0, sl, ls[p]] * c_in[p]).astype(BF16) for p in ps]
        at = [stack2(a_ref[0, sl, ls[p]] * c_ex[p]).astype(BF16) for p in ps]
        bt = [stack2(b_ref[0, sl, ls[p]] * c_inv[p]).astype(BF16) for p in ps]
        kt = [stack2(k_ref[0, sl, ls[p]] * c_inv[p]).astype(BF16) for p in ps]
        v2 = [stack2(v_ref[0, sl, ls[p]]).astype(BF16) for p in ps]
        s = [s_scr[p] for p in ps]
        sb = [s[p].astype(BF16) for p in ps]
        l_ab = [jnp.where(strict, _bdot(at[p], bt[p], _NT), 0.0) for p in ps]
        l_ak = [jnp.where(strict, _bdot(at[p], kt[p], _NT), 0.0).astype(BF16) for p in ps]
        m_rb = [jnp.where(incl, _bdot(rt[p], bt[p], _NT), 0.0).astype(BF16) for p in ps]
        m_rk = [jnp.where(incl, _bdot(rt[p], kt[p], _NT), 0.0).astype(BF16) for p in ps]
        rhs = [_bdot(at[p], sb[p], _NT) + _bdot(l_ak[p], v2[p]) for p in ps]
        y0 = [_bdot(rt[p], sb[p], _NT) + _bdot(m_rk[p], v2[p]) for p in ps]
        tinv = [eye + jnp.where(merge[0], l_ab[p], 0.0) for p in ps]
        for lv in range(1, n_levels):
            tb = [tinv[p].astype(BF16) for p in ps]
            off = [_bdot(jnp.where(merge[lv], l_ab[p], 0.0), tb[p]) for p in ps]
            tinv = [tinv[p] + _bdot(tb[p], off[p]) for p in ps]
        u = [_bdot(tinv[p], rhs[p]).astype(BF16) for p in ps]
        y2 = [y0[p] + _bdot(m_rb[p], u[p]) for p in ps]
        upd = [_bdot(u[p], bt[p], _TN) + _bdot(v2[p], kt[p], _TN) for p in ps]
        for p in ps:
            y_ref[0, sl, ls[p]] = y2[p][:chunk] + y2[p][chunk:]
            s_scr[p] = (s[p] + upd[p]) * c_in[p][chunk - 1:chunk, :]
        return carry

    lax.fori_loop(0, n_chunks, one_chunk, 0)

    @pl.when(tstep == pl.num_programs(2) - 1)
    def _():
        for p in range(pairs):
            st_ref[0, 2 * p] = s_scr[p, 0:hs, 0:hs]
            st_ref[0, 2 * p + 1] = s_scr[p, hs:2 * hs, hs:2 * hs]


def _wkv(r, lw, k, v, a, b, s0, layer):
    bsz, t, d = r.shape
    t_real = t
    if t % SUBLANES:
        pad = SUBLANES - t % SUBLANES
        r, lw, k, v, a, b = (jnp.pad(z, ((0, 0), (0, pad), (0, 0))) for z in (r, lw, k, v, a, b))
        t += pad
    chunk = _tile(t, WKV_CHUNK, SUBLANES)
    tt = _tile(t, WKV_TIME_TILE, chunk)
    n_pairs = d // LANES
    pairs = _tile(n_pairs, WKV_PAIRS, 1)
    n_levels = (chunk - 1).bit_length()
    assert chunk == 1 << n_levels, "the block-merge inverse needs a power-of-two chunk"
    seq = pl.BlockSpec((1, tt, pairs * LANES), lambda i, p, j: (i, j, p))
    st_shape = (2 * pairs, HEAD_SIZE, HEAD_SIZE)
    y, s_t = pl.pallas_call(
        functools.partial(_wkv_body, chunk=chunk, pairs=pairs, n_chunks=tt // chunk, n_levels=n_levels),
        grid=(bsz, n_pairs // pairs, t // tt),
        in_specs=[seq] * 6 + [pl.BlockSpec((None, 1) + st_shape, lambda i, p, j: (layer, i, p, 0, 0))],
        out_specs=[seq, pl.BlockSpec((1,) + st_shape, lambda i, p, j: (i, p, 0, 0))],
        out_shape=[jax.ShapeDtypeStruct((bsz, t, d), F32),
                   jax.ShapeDtypeStruct((bsz, 2 * n_pairs, HEAD_SIZE, HEAD_SIZE), F32)],
        scratch_shapes=[pltpu.VMEM((pairs, LANES, LANES), F32)],
        compiler_params=_params("arbitrary", "arbitrary", "arbitrary"),
        name="wkv",
    )(r, lw, k, v, a, b, s0)
    return y[:, :t_real], s_t


def _dwconv_body(u_ref, buf_ref, w_ref, b_ref, lnw_ref, lnb_ref, o_ref, win_ref, *, tt, width, halo):
    @pl.when(pl.program_id(1) == 0)
    def _():
        win_ref[0:halo, :] = buf_ref[0]

    win_ref[halo:halo + tt, :] = u_ref[0]
    first = halo - (width - 1)
    z = b_ref[...] + w_ref[0:1, :] * win_ref[first:first + tt, :]
    for j in range(1, width):
        z = z + w_ref[j:j + 1, :] * win_ref[first + j:first + j + tt, :]
    mu = jnp.mean(z, axis=-1, keepdims=True)
    zc = z - mu
    var = jnp.mean(zc * zc, axis=-1, keepdims=True)
    zn = zc * lax.rsqrt(var + CONV_LN_EPS) * lnw_ref[...] + lnb_ref[...]
    o_ref[0] = (zn * jax.nn.sigmoid(zn)).astype(o_ref.dtype)
    win_ref[0:halo, :] = win_ref[tt:tt + halo, :]


def _dwconv_ln_silu(u, buf, w_dw, b_dw, ln_w, ln_b):
    bsz, t, d = u.shape
    width = w_dw.shape[0]
    halo = -(-(width - 1) // SUBLANES) * SUBLANES
    tt = _tile(t, 128, SUBLANES)
    assert tt >= halo or tt == t
    bufp = jnp.pad(buf, ((0, 0), (halo - (width - 1), 0), (0, 0)))
    vec = pl.BlockSpec((1, d), lambda i, j: (0, 0))
    return pl.pallas_call(
        functools.partial(_dwconv_body, tt=tt, width=width, halo=halo),
        grid=(bsz, t // tt),
        in_specs=[pl.BlockSpec((1, tt, d), lambda i, j: (i, j, 0)),
                  pl.BlockSpec((1, halo, d), lambda i, j: (i, 0, 0)),
                  pl.BlockSpec((width, d), lambda i, j: (0, 0)), vec, vec, vec],
        out_specs=pl.BlockSpec((1, tt, d), lambda i, j: (i, j, 0)),
        out_shape=jax.ShapeDtypeStruct((bsz, t, d), BF16),
        scratch_shapes=[pltpu.VMEM((halo + max(tt, halo), d), F32)],
        compiler_params=_params("arbitrary", "arbitrary"),
        name="dwconv_ln_silu",
    )(u, bufp, w_dw, b_dw.reshape(1, d), ln_w.reshape(1, d), ln_b.reshape(1, d))


def _pool_body(h_ref, buf_ref, w_ref, sc_ref, o_ref, win_ref, wb_ref, *, tt, halo, start_pos, group):
    tstep = pl.program_id(1)

    @pl.when((pl.program_id(0) == 0) & (tstep == 0))
    def _():
        wb_ref[...] = w_ref[...].astype(BF16)

    @pl.when(tstep == 0)
    def _():
        win_ref[0:halo, :] = buf_ref[0]

    win_ref[halo:halo + tt, :] = h_ref[0]
    pos = start_pos + tstep * tt + lax.broadcasted_iota(jnp.int32, (tt, 1), 0)
    for gi, win in enumerate(POOL_WINDOWS):
        cs = slice(gi * group, (gi + 1) * group)
        tot = win_ref[halo:halo + tt, cs]
        for back in range(1, win):
            tot = tot + win_ref[halo - back:halo - back + tt, cs]
        cnt = jnp.minimum(pos + 1, win).astype(F32)
        dlt = tot / cnt - win_ref[halo:halo + tt, cs]
        o_ref[0, :, cs] = jnp.dot(dlt.astype(BF16), wb_ref[gi], preferred_element_type=F32) * sc_ref[:, cs]
    win_ref[0:halo, :] = win_ref[tt:tt + halo, :]


def _pool_mixer(h, buf, start_pos, w_grp, scale):
    bsz, t, d = h.shape
    ng, group, _ = w_grp.shape
    halo = -(-POOL_BUF // SUBLANES) * SUBLANES
    tt = _tile(t, 256, SUBLANES)
    assert tt >= halo or tt == t
    bufp = jnp.pad(buf, ((0, 0), (halo - POOL_BUF, 0), (0, 0)))
    return pl.pallas_call(
        functools.partial(_pool_body, tt=tt, halo=halo, start_pos=start_pos, group=group),
        grid=(bsz, t // tt),
        in_specs=[pl.BlockSpec((1, tt, d), lambda i, j: (i, j, 0)),
                  pl.BlockSpec((1, halo, d), lambda i, j: (i, 0, 0)),
                  pl.BlockSpec((ng, group, group), lambda i, j: (0, 0, 0)),
                  pl.BlockSpec((1, d), lambda i, j: (0, 0))],
        out_specs=pl.BlockSpec((1, tt, d), lambda i, j: (i, j, 0)),
        out_shape=jax.ShapeDtypeStruct((bsz, t, d), F32),
        scratch_shapes=[pltpu.VMEM((halo + max(tt, halo), d), F32), pltpu.VMEM((ng, group, group), BF16)],
        compiler_params=_params("arbitrary", "arbitrary"),
        name="pool_mixer",
    )(h, bufp, w_grp, scale.reshape(1, d))


def _keep_expert_rows(meta_ref, i, new, out_ref):
    rows = lax.broadcasted_iota(jnp.int32, new.shape, 0)
    mine = (rows >= meta_ref[5, i]) & (rows < meta_ref[6, i])

    @pl.when(meta_ref[7, i] == 1)
    def _():
        out_ref[...] = jnp.where(mine, new, 0.0).astype(out_ref.dtype)

    @pl.when(meta_ref[7, i] == 0)
    def _():
        out_ref[...] = jnp.where(mine, new.astype(out_ref.dtype), out_ref[...])


def _stream_run_weights(meta_ref, cnt_ref, start_copy, stage, sem, cast):
    j, i = pl.program_id(0), pl.program_id(1)
    n_used, n_runs = cnt_ref[0], cnt_ref[1]

    @pl.when((i < n_used) & (meta_ref[1, i] == 1))
    def _():
        run = meta_ref[2, i]
        g = j * n_runs + run
        slot = lax.rem(g, 2)

        @pl.when(g == 0)
        def _():
            start_copy(meta_ref[0, 0], 0, 0)

        pltpu.make_async_copy(stage.at[slot], stage.at[slot], sem.at[slot]).wait()
        last_run = run + 1 == n_runs

        @pl.when(jnp.logical_not(last_run) | (j + 1 < pl.num_programs(0)))
        def _():
            start_copy(meta_ref[3, i], jnp.where(last_run, j + 1, j), 1 - slot)

        cast(slot)


def _moe_up_body(meta_ref, cnt_ref, x_ref, w_hbm, bg_ref, bu_ref, h_ref, stage, wb_ref, sem, *, layer, tf, f):
    def start_copy(e, tile, slot):
        col = pl.multiple_of(tile * tf, tf)
        w_e = w_hbm.at[layer, e]
        pltpu.make_async_copy(w_e.at[:, pl.ds(col, tf)], stage.at[slot, 0],
                              sem.at[slot]).start(priority=WEIGHT_DMA_PRIORITY)
        pltpu.make_async_copy(w_e.at[:, pl.ds(f + col, tf)], stage.at[slot, 1],
                              sem.at[slot]).start(priority=WEIGHT_DMA_PRIORITY)

    def cast(slot):
        wb_ref[0] = stage[slot, 0].astype(BF16)
        wb_ref[1] = stage[slot, 1].astype(BF16)

    _stream_run_weights(meta_ref, cnt_ref, start_copy, stage, sem, cast)
    i = pl.program_id(1)

    @pl.when(i < cnt_ref[0])
    def _():
        x = x_ref[...]
        g = jnp.dot(x, wb_ref[0], preferred_element_type=F32) + bg_ref[...]
        u = jnp.dot(x, wb_ref[1], preferred_element_type=F32) + bu_ref[...]
        g = jnp.minimum(g, SWIGLU_LIMIT)
        u = jnp.clip(u, -SWIGLU_LIMIT, SWIGLU_LIMIT)
        _keep_expert_rows(meta_ref, i, (u + 1.0) * g * jax.nn.sigmoid(SWIGLU_ALPHA * g), h_ref)


def _moe_down_body(meta_ref, cnt_ref, h_ref, w_hbm, b_ref, y_ref, stage, wb_ref, sem, *, layer, tn):
    def start_copy(e, tile, slot):
        col = pl.multiple_of(tile * tn, tn)
        pltpu.make_async_copy(w_hbm.at[layer, e].at[:, pl.ds(col, tn)], stage.at[slot],
                              sem.at[slot]).start(priority=WEIGHT_DMA_PRIORITY)

    def cast(slot):
        wb_ref[...] = stage[slot].astype(BF16)

    _stream_run_weights(meta_ref, cnt_ref, start_copy, stage, sem, cast)
    i = pl.program_id(1)

    @pl.when(i < cnt_ref[0])
    def _():
        y = jnp.dot(h_ref[...], wb_ref[...], preferred_element_type=F32) + b_ref[...]
        _keep_expert_rows(meta_ref, i, y, y_ref)


def _moe_items(counts, n_blocks, bm):
    n_exp = counts.shape[0]
    n_items_max = n_blocks + n_exp - 1
    end = jnp.cumsum(counts)
    start = end - counts
    first_blk = start // bm
    n_e = jnp.where(counts > 0, (end - 1) // bm - first_blk + 1, 0)
    item_end = jnp.cumsum(n_e)
    n_items = item_end[-1]
    w = jnp.arange(n_items_max, dtype=jnp.int32)
    live = w < n_items
    wc = jnp.minimum(w, n_items - 1)
    e = jnp.minimum(jnp.searchsorted(item_end, wc, side="right"), n_exp - 1).astype(jnp.int32)
    blk = first_blk[e] + wc - (item_end[e] - n_e[e])
    lo = jnp.where(live, jnp.clip(start[e] - blk * bm, 0, bm), 0)
    hi = jnp.where(live, jnp.clip(end[e] - blk * bm, 0, bm), 0)
    opens = live & ((w == 0) | (e != jnp.roll(e, 1)))
    run = jnp.cumsum(opens.astype(jnp.int32)) - 1
    starts = jnp.where(opens, w, n_items_max)
    later = jnp.concatenate([lax.cummin(starts, reverse=True)[1:], jnp.full((1,), n_items_max, jnp.int32)])
    nxt_e = e[jnp.where(later < n_items_max, later, 0)]
    first_visit = live & ((w == 0) | (blk != jnp.roll(blk, 1)))
    meta = jnp.stack([e, opens, run, nxt_e, blk, lo, hi, first_visit]).astype(jnp.int32)
    cnt = jnp.stack([n_items, jnp.sum(opens.astype(jnp.int32))]).astype(jnp.int32)
    return meta, cnt, n_items_max


def _moe_experts(xs, counts, w_gu, b_gu, w_down, b_down, layer):
    n_rows, d = xs.shape
    _, n_exp, _, f2 = w_gu.shape
    f = f2 // 2
    bm = MOE_ROWS
    meta, cnt, n_items = _moe_items(counts, n_rows // bm, bm)
    tf = _tile(f, MOE_COLS, LANES)
    nf = f // tf
    b_gu3 = b_gu[layer].reshape(n_exp, 1, f2)
    hidden = pl.pallas_call(
        functools.partial(_moe_up_body, layer=layer, tf=tf, f=f),
        grid_spec=pltpu.PrefetchScalarGridSpec(
            num_scalar_prefetch=2,
            grid=(nf, n_items),
            in_specs=[pl.BlockSpec((bm, d), lambda j, i, mt, ct: (mt[4, i], 0)),
                      pl.BlockSpec(memory_space=pl.ANY),
                      pl.BlockSpec((None, 1, tf), lambda j, i, mt, ct: (mt[0, i], 0, j)),
                      pl.BlockSpec((None, 1, tf), lambda j, i, mt, ct: (mt[0, i], 0, j + nf))],
            out_specs=pl.BlockSpec((bm, tf), lambda j, i, mt, ct: (mt[4, i], j)),
            scratch_shapes=[pltpu.VMEM((2, 2, d, tf), F32), pltpu.VMEM((2, d, tf), BF16),
                            pltpu.SemaphoreType.DMA((2,))]),
        out_shape=jax.ShapeDtypeStruct((n_rows, f), BF16),
        compiler_params=_params("arbitrary", "arbitrary"),
        name="moe_up",
    )(meta, cnt, xs, w_gu, b_gu3, b_gu3)
    tn = _tile(d, MOE_COLS, LANES)
    return pl.pallas_call(
        functools.partial(_moe_down_body, layer=layer, tn=tn),
        grid_spec=pltpu.PrefetchScalarGridSpec(
            num_scalar_prefetch=2,
            grid=(d // tn, n_items),
            in_specs=[pl.BlockSpec((bm, f), lambda j, i, mt, ct: (mt[4, i], 0)),
                      pl.BlockSpec(memory_space=pl.ANY),
                      pl.BlockSpec((None, 1, tn), lambda j, i, mt, ct: (mt[0, i], 0, j))],
            out_specs=pl.BlockSpec((bm, tn), lambda j, i, mt, ct: (mt[4, i], j)),
            scratch_shapes=[pltpu.VMEM((2, f, tn), F32), pltpu.VMEM((f, tn), BF16),
                            pltpu.SemaphoreType.DMA((2,))]),
        out_shape=jax.ShapeDtypeStruct((n_rows, d), F32),
        compiler_params=_params("arbitrary", "arbitrary"),
        name="moe_down",
    )(meta, cnt, hidden, w_down, b_down[layer].reshape(n_exp, 1, d))


def _moe(hf, w_router, b_router, w_gu, b_gu, w_down, b_down, layer):
    n, d = hf.shape
    n_exp = w_router.shape[2]
    bm = MOE_ROWS
    logits = _matmul(hf, w_router, b_router, layer, exact=True)
    top_v, top_e = lax.top_k(logits, TOP_K)
    gates = jax.nn.softmax(top_v, axis=-1)
    n_assign = n * TOP_K
    flat_e = top_e.reshape(n_assign).astype(jnp.int32)
    counts = jnp.sum((flat_e[:, None] == jnp.arange(n_exp, dtype=jnp.int32)[None, :]).astype(jnp.int32), axis=0)
    ids = jnp.arange(n_assign, dtype=jnp.int32)
    _, order = lax.sort((flat_e, ids), num_keys=1, is_stable=True)
    _, pos = lax.sort((order, ids), num_keys=1)
    n_rows = -(-n_assign // bm) * bm
    row_tok = jnp.pad(order // TOP_K, (0, n_rows - n_assign))
    xs = hf.astype(BF16)[row_tok]
    y = _moe_experts(xs, counts, w_gu, b_gu, w_down, b_down, layer)
    return y, pos.reshape(n, TOP_K), gates


def _combine_body(pos_ref, nxt_ref, y_hbm, eg_ref, x_ref, gt_ref, g_ref, *rest, tm, final):
    if final:
        o_ref, buf, sem = rest
    else:
        sh_ref, sc_ref, xo_ref, h_ref, buf, sem = rest
    step = pl.program_id(0) * pl.num_programs(1) + pl.program_id(1)
    n_steps = pl.num_programs(0) * pl.num_programs(1)
    slot = lax.rem(step, 2)

    def start_block(idx_ref, dst_slot):
        def body(r, carry):
            for k in range(TOP_K):
                row = idx_ref[0, r * TOP_K + k]
                pltpu.make_async_copy(y_hbm.at[pl.ds(row, 1)], buf.at[dst_slot, k, pl.ds(r, 1)],
                                      sem.at[dst_slot]).start(priority=k % 2)
            return carry
        lax.fori_loop(0, tm, body, 0, unroll=8)

    @pl.when(step == 0)
    def _():
        start_block(pos_ref, 0)

    @pl.when(step + 1 < n_steps)
    def _():
        start_block(nxt_ref, 1 - slot)

    pltpu.make_async_copy(buf.at[slot], buf.at[slot], sem.at[slot]).wait()
    eg = eg_ref[0]
    moe = eg[:, 0:1] * buf[slot, 0]
    for k in range(1, TOP_K):
        moe = moe + eg[:, k:k + 1] * buf[slot, k]
    x = x_ref[0] + gt_ref[0] * moe
    if final:
        o_ref[0] = (x * lax.rsqrt(jnp.mean(x * x, axis=-1, keepdims=True) + RMS_EPS)) * g_ref[...]
    else:
        xo_ref[0] = x
        h_ref[0] = _rms_mod(x, g_ref[...], sh_ref[0], sc_ref[0]).astype(h_ref.dtype)


def _moe_combine(y, pos, expert_gates, x, gate, g, shift=None, scale=None):
    bsz, t, d = x.shape
    final = shift is None
    tm = _tile(t, 128, SUBLANES)
    nt = t // tm
    n_steps = bsz * nt
    pos3 = pos.astype(jnp.int32).reshape(n_steps, 1, tm * TOP_K)
    idx = pl.BlockSpec((None, 1, tm * TOP_K), lambda i, j: (i * nt + j, 0, 0), memory_space=pltpu.SMEM)
    nxt = pl.BlockSpec((None, 1, tm * TOP_K), lambda i, j: (jnp.minimum(i * nt + j + 1, n_steps - 1), 0, 0),
                       memory_space=pltpu.SMEM)
    row = pl.BlockSpec((1, tm, d), lambda i, j: (i, j, 0))
    vec = pl.BlockSpec((1, d), lambda i, j: (0, 0))
    in_specs = [idx, nxt, pl.BlockSpec(memory_space=pl.ANY), pl.BlockSpec((1, tm, TOP_K), lambda i, j: (i, j, 0)),
                row, _mod_spec(gate, tm), vec]
    args = [pos3, pos3, y, expert_gates.reshape(bsz, t, TOP_K), x, gate, g.reshape(1, d)]
    if final:
        out_specs, out_shape = row, jax.ShapeDtypeStruct((bsz, t, d), F32)
    else:
        in_specs += [_mod_spec(shift, tm), _mod_spec(scale, tm)]
        args += [shift, scale]
        out_specs = [row, row]
        out_shape = [jax.ShapeDtypeStruct((bsz, t, d), F32), jax.ShapeDtypeStruct((bsz, t, d), F32)]
    return pl.pallas_call(
        functools.partial(_combine_body, tm=tm, final=final),
        grid=(bsz, nt),
        in_specs=in_specs,
        out_specs=out_specs,
        out_shape=out_shape,
        scratch_shapes=[pltpu.VMEM((2, TOP_K, tm, d), F32), pltpu.SemaphoreType.DMA((2,))],
        compiler_params=_params("arbitrary", "arbitrary"),
        name="moe_combine",
    )(*args)


def _rwkv7(h, shift_prev, s0, v_first, p, j):
    bsz, t, d = h.shape
    n = bsz * t
    heads = d // HEAD_SIZE
    prev = jnp.concatenate([shift_prev[:, None], h[:, :-1]], axis=1)
    xx = prev - h
    mix = p["rwkv_mix"][j]
    xr, xw, xk, xv, xa, xg = ((h + xx * mix[m]).astype(BF16).reshape(n, d) for m in range(6))
    r = _matmul(xr, p["rwkv_wr"], layer=j)
    k = _matmul(xk, p["rwkv_wk"], layer=j)
    v = _matmul(xv, p["rwkv_wv"], layer=j)
    if j == 0:
        v_first = v
    else:
        v_lora = _matmul(_matmul(xv, p["rwkv_v1"], layer=j - 1, out_dtype=BF16), p["rwkv_v2"], layer=j - 1)
        v = v + (v_first - v) * jax.nn.sigmoid(p["rwkv_v0"][j - 1] + v_lora)
    w_lora = _matmul(_matmul(xw, p["rwkv_w1"], layer=j, act="tanh", out_dtype=BF16), p["rwkv_w2"], layer=j)
    logw = -jax.nn.softplus(-(p["rwkv_w0"][j] + w_lora)) - 0.5
    log_decay = -jnp.exp(logw)
    a_lora = _matmul(_matmul(xa, p["rwkv_a1"], layer=j, out_dtype=BF16), p["rwkv_a2"], layer=j)
    a = jax.nn.sigmoid(p["rwkv_a0"][j] + a_lora)
    g = _matmul(_matmul(xg, p["rwkv_g1"], layer=j, act="sigmoid", out_dtype=BF16), p["rwkv_g2"], layer=j)
    kk = (k * p["rwkv_kk"][j]).reshape(n, heads, HEAD_SIZE)
    kk = kk / jnp.maximum(jnp.sqrt(jnp.sum(kk * kk, axis=-1, keepdims=True)), 1e-12)
    kk = kk.reshape(n, d)
    k = k * (1.0 + (a - 1.0) * p["rwkv_ka"][j])
    seq = lambda z: z.reshape(bsz, t, d)
    y, s_t = _wkv(seq(r), seq(log_decay), seq(k), seq(v), seq(-kk), seq(kk * a), s0, j)
    y = y.reshape(n, heads, HEAD_SIZE)
    mu = jnp.mean(y, axis=-1, keepdims=True)
    var = jnp.mean(jnp.square(y - mu), axis=-1, keepdims=True)
    y = ((y - mu) * lax.rsqrt(var + LN_X_EPS)).reshape(n, d) * p["rwkv_lnx_w"][j] + p["rwkv_lnx_b"][j]
    rk = (r * k).reshape(n, heads, HEAD_SIZE) * p["rwkv_rk"][j]
    bonus = (jnp.sum(rk, axis=-1, keepdims=True) * v.reshape(n, heads, HEAD_SIZE)).reshape(n, d)
    out = _matmul(((y + bonus) * g).astype(BF16), p["rwkv_wo"], layer=j)
    return out.reshape(bsz, t, d), h[:, -1], s_t, v_first


def _conformer_conv(h, buf, p, j):
    bsz, t, d = h.shape
    width = p["conv_w_dw"].shape[1]
    u = _glu_matmul(h.reshape(bsz * t, d), p["conv_w_in"], p["conv_b_in"], j).reshape(bsz, t, -1)
    z = _dwconv_ln_silu(u, buf, p["conv_w_dw"][j], p["conv_b_dw"][j], p["conv_ln_w"][j], p["conv_ln_b"][j])
    out = _matmul(z.reshape(bsz * t, -1), p["conv_w_out"], p["conv_b_out"], j).reshape(bsz, t, d)
    new_buf = jnp.concatenate([buf, u], axis=1)[:, -(width - 1):]
    return out, new_buf


def _pool(h, buf, start_pos, p, j):
    out = _pool_mixer(h, buf, start_pos, p["pool_w"][j], p["pool_scale"][j])
    return out, jnp.concatenate([buf, h], axis=1)[:, -POOL_BUF:]


def kernel(x_prompt, x_sample, c_prompt, c_sample, state_wkv, state_shift, state_conv, state_pool, ada_w, ada_b, norm_mix, norm_ffn, final_norm, rwkv_mix, rwkv_wr, rwkv_wk, rwkv_wv, rwkv_wo, rwkv_w0, rwkv_w1, rwkv_w2, rwkv_a0, rwkv_a1, rwkv_a2, rwkv_v0, rwkv_v1, rwkv_v2, rwkv_g1, rwkv_g2, rwkv_kk, rwkv_ka, rwkv_rk, rwkv_lnx_w, rwkv_lnx_b, conv_w_in, conv_b_in, conv_w_dw, conv_b_dw, conv_ln_w, conv_ln_b, conv_w_out, conv_b_out, pool_w, pool_scale, moe_w_router, moe_b_router, moe_w_gu, moe_b_gu, moe_w_down, moe_b_down):
    p = dict(rwkv_mix=rwkv_mix, rwkv_wr=rwkv_wr, rwkv_wk=rwkv_wk, rwkv_wv=rwkv_wv, rwkv_wo=rwkv_wo,
             rwkv_w0=rwkv_w0, rwkv_w1=rwkv_w1, rwkv_w2=rwkv_w2, rwkv_a0=rwkv_a0, rwkv_a1=rwkv_a1,
             rwkv_a2=rwkv_a2, rwkv_v0=rwkv_v0, rwkv_v1=rwkv_v1, rwkv_v2=rwkv_v2, rwkv_g1=rwkv_g1,
             rwkv_g2=rwkv_g2, rwkv_kk=rwkv_kk, rwkv_ka=rwkv_ka, rwkv_rk=rwkv_rk, rwkv_lnx_w=rwkv_lnx_w,
             rwkv_lnx_b=rwkv_lnx_b, conv_w_in=conv_w_in, conv_b_in=conv_b_in, conv_w_dw=conv_w_dw,
             conv_b_dw=conv_b_dw, conv_ln_w=conv_ln_w, conv_ln_b=conv_ln_b, conv_w_out=conv_w_out,
             conv_b_out=conv_b_out, pool_w=pool_w, pool_scale=pool_scale)
    depth = ada_w.shape[0]
    bp, tp, d = x_prompt.shape
    bs, ts, _ = x_sample.shape
    heads = d // HEAD_SIZE
    n_p, n_s = bp * tp, bs * ts
    width = conv_w_dw.shape[1]

    c_act = jax.nn.silu(jnp.concatenate([c_prompt, c_sample], axis=0))
    mods = []
    for i in range(depth):
        m = _matmul(c_act, ada_w, ada_b, i)
        six = jnp.split(m, 6, axis=-1)
        mods.append(([z[:bp, None, :] for z in six],
                     [jnp.repeat(z[bp:], ts, axis=0)[None] for z in six]))

    groups = [
        dict(x=x_prompt, b=bp, t=tp, start=0,
             wkv=jnp.zeros((state_wkv.shape[0], bp, heads, HEAD_SIZE, HEAD_SIZE), F32),
             shift=jnp.zeros((state_shift.shape[0], bp, d), F32),
             conv=jnp.zeros((state_conv.shape[0], bp, width - 1, conv_w_dw.shape[2]), F32),
             pool=jnp.zeros((state_pool.shape[0], bp, POOL_BUF, d), F32)),
        dict(x=x_sample.reshape(1, n_s, d), b=bs, t=ts, start=PAST_LEN,
             wkv=state_wkv, shift=state_shift, conv=state_conv, pool=state_pool),
    ]
    for gi, grp in enumerate(groups):
        grp.update(wkv_out=[], shift_out=[], conv_out=[], pool_out=[], v_first=None)
        sh1, sc1 = mods[0][gi][0], mods[0][gi][1]
        grp["h"] = _normmod(grp["x"], norm_mix[0], sh1, sc1, F32)

    for i in range(depth):
        kind = i % 3
        hfs = []
        for gi, grp in enumerate(groups):
            sh1, sc1, gt1, sh2, sc2, gt2 = mods[i][gi]
            h = grp["h"].reshape(grp["b"], grp["t"], d)
            if kind == 0:
                j = len(grp["wkv_out"])
                out, last, s_t, grp["v_first"] = _rwkv7(h, grp["shift"][j], grp["wkv"], grp["v_first"], p, j)
                grp["wkv_out"].append(s_t)
                grp["shift_out"].append(last)
            elif kind == 1:
                j = len(grp["conv_out"])
                out, buf = _conformer_conv(h, grp["conv"][j], p, j)
                grp["conv_out"].append(buf)
            else:
                j = len(grp["pool_out"])
                out, buf = _pool(h, grp["pool"][j], grp["start"], p, j)
                grp["pool_out"].append(buf)
            grp["x"], hf = _resnormmod(grp["x"], out.reshape(grp["x"].shape), gt1, norm_ffn[i], sh2, sc2, F32)
            hfs.append(hf.reshape(-1, d))
        y, pos, egates = _moe(jnp.concatenate(hfs, axis=0), moe_w_router, moe_b_router, moe_w_gu, moe_b_gu,
                              moe_w_down, moe_b_down, i)
        for gi, grp in enumerate(groups):
            gt2 = mods[i][gi][5]
            part = slice(0, n_p) if gi == 0 else slice(n_p, None)
            if i + 1 < depth:
                sh1, sc1 = mods[i + 1][gi][0], mods[i + 1][gi][1]
                grp["x"], grp["h"] = _moe_combine(y, pos[part], egates[part], grp["x"], gt2, norm_mix[i + 1],
                                                  sh1, sc1)
            else:
                grp["x"] = _moe_combine(y, pos[part], egates[part], grp["x"], gt2, final_norm)

    gp, gs = groups
    return (gp["x"], gs["x"].reshape(bs, ts, d),
            jnp.stack(gp["wkv_out"]), jnp.stack(gp["shift_out"]), jnp.stack(gp["conv_out"]), jnp.stack(gp["pool_out"]),
            jnp.stack(gs["wkv_out"]), jnp.stack(gs["shift_out"]), jnp.stack(gs["conv_out"]), jnp.stack(gs["pool_out"]))
```

```python
import functools

import jax
import jax.numpy as jnp
from jax import lax
from jax.experimental import pallas as pl
from jax.experimental.pallas import tpu as pltpu

F32 = jnp.float32
BF16 = jnp.bfloat16
HIGHEST = lax.Precision.HIGHEST

HEAD_SIZE = 64
TOP_K = 4
POOL_WINDOWS = (2, 4, 8, 16)
POOL_BUF = max(POOL_WINDOWS) - 1
PAST_LEN = 16384
RMS_EPS = 1e-5
LN_X_EPS = 64e-5
CONV_LN_EPS = 1e-5
SWIGLU_LIMIT = 7.0
SWIGLU_ALPHA = 1.702

VMEM_LIMIT_BYTES = 56 * 1024 * 1024
SUBLANES = 8
LANES = 128
WKV_CHUNK = 64
WKV_PAIRS = 16
WKV_TIME_TILE = 128
MOE_ROWS = 256
MOE_COLS = 1024
WEIGHT_DMA_PRIORITY = 1


def _tile(n, pref, mult):
    if n <= pref:
        return n
    t = pref - pref % mult
    while t >= mult:
        if n % t == 0:
            return t
        t -= mult
    return n


def _params(*sem):
    return pltpu.CompilerParams(dimension_semantics=sem, vmem_limit_bytes=VMEM_LIMIT_BYTES)


def _bdot(a, b, dims=(((1,), (0,)), ((), ()))):
    return lax.dot_general(a.astype(BF16), b.astype(BF16), dims, preferred_element_type=F32)


_NT = (((1,), (1,)), ((), ()))
_TN = (((0,), (0,)), ((), ()))


def _apply_act(acc, act):
    if act == "tanh":
        return jnp.tanh(acc)
    if act == "sigmoid":
        return jax.nn.sigmoid(acc)
    assert act is None, act
    return acc


def _mm_body(x_ref, w_ref, b_ref, o_ref, wb_ref, *, act, exact):
    @pl.when(pl.program_id(1) == 0)
    def _():
        wb_ref[...] = w_ref[...].astype(wb_ref.dtype)

    if exact:
        acc = jnp.dot(x_ref[...].astype(F32), wb_ref[...], precision=HIGHEST, preferred_element_type=F32)
    else:
        acc = jnp.dot(x_ref[...].astype(BF16), wb_ref[...], preferred_element_type=F32)
    o_ref[...] = _apply_act(acc + b_ref[...], act).astype(o_ref.dtype)


def _matmul(x, w, b=None, layer=0, act=None, out_dtype=F32, exact=False, tm=512, tn=1024):
    m, k = x.shape
    n = w.shape[2]
    tm = _tile(m, tm, SUBLANES)
    tn = _tile(n, tn, LANES)
    b = jnp.zeros((1, n), F32) if b is None else b[layer].reshape(1, n)
    return pl.pallas_call(
        functools.partial(_mm_body, act=act, exact=exact),
        grid=(n // tn, m // tm),
        in_specs=[pl.BlockSpec((tm, k), lambda j, i: (i, 0)),
                  pl.BlockSpec((None, k, tn), lambda j, i: (layer, 0, j)),
                  pl.BlockSpec((1, tn), lambda j, i: (0, j))],
        out_specs=pl.BlockSpec((tm, tn), lambda j, i: (i, j)),
        out_shape=jax.ShapeDtypeStruct((m, n), out_dtype),
        scratch_shapes=[pltpu.VMEM((k, tn), F32 if exact else BF16)],
        compiler_params=_params("arbitrary", "arbitrary"),
        name="matmul",
    )(x, w, b)


def _glu_body(x_ref, wa_ref, wg_ref, ba_ref, bg_ref, o_ref, wab_ref, wgb_ref):
    @pl.when(pl.program_id(1) == 0)
    def _():
        wab_ref[...] = wa_ref[...].astype(BF16)
        wgb_ref[...] = wg_ref[...].astype(BF16)

    x = x_ref[...].astype(BF16)
    a = jnp.dot(x, wab_ref[...], preferred_element_type=F32) + ba_ref[...]
    g = jnp.dot(x, wgb_ref[...], preferred_element_type=F32) + bg_ref[...]
    o_ref[...] = (a * jax.nn.sigmoid(g)).astype(o_ref.dtype)


def _glu_matmul(x, w, b, layer, tm=512, tn=512):
    m, k = x.shape
    n = w.shape[2] // 2
    tm = _tile(m, tm, SUBLANES)
    tn = _tile(n, tn, LANES)
    nj = n // tn
    b = b[layer].reshape(1, 2 * n)
    return pl.pallas_call(
        _glu_body,
        grid=(nj, m // tm),
        in_specs=[pl.BlockSpec((tm, k), lambda j, i: (i, 0)),
                  pl.BlockSpec((None, k, tn), lambda j, i: (layer, 0, j)),
                  pl.BlockSpec((None, k, tn), lambda j, i: (layer, 0, j + nj)),
                  pl.BlockSpec((1, tn), lambda j, i: (0, j)),
                  pl.BlockSpec((1, tn), lambda j, i: (0, j + nj))],
        out_specs=pl.BlockSpec((tm, tn), lambda j, i: (i, j)),
        out_shape=jax.ShapeDtypeStruct((m, n), F32),
        scratch_shapes=[pltpu.VMEM((k, tn), BF16), pltpu.VMEM((k, tn), BF16)],
        compiler_params=_params("arbitrary", "arbitrary"),
        name="glu_matmul",
    )(x, w, w, b, b)


def _rms_mod(x, g, shift, scale):
    y = x * lax.rsqrt(jnp.mean(x * x, axis=-1, keepdims=True) + RMS_EPS)
    return (y * g) * (1.0 + scale) + shift


def _normmod_body(x_ref, g_ref, sh_ref, sc_ref, h_ref):
    h_ref[0] = _rms_mod(x_ref[0], g_ref[...], sh_ref[0], sc_ref[0]).astype(h_ref.dtype)


def _resnormmod_body(x_ref, y_ref, gt_ref, g_ref, sh_ref, sc_ref, tokens_hbm, xo_ref, h_ref):
    del tokens_hbm
    x = x_ref[0] + gt_ref[0] * y_ref[0]
    xo_ref[0] = x
    h_ref[...] = _rms_mod(x, g_ref[...], sh_ref[0], sc_ref[0])


def _mod_spec(mod, tt):
    if mod.shape[1] == 1:
        return pl.BlockSpec((1, 1, mod.shape[2]), lambda b, t: (b, 0, 0))
    return pl.BlockSpec((1, tt, mod.shape[2]), lambda b, t: (b, t, 0))


def _normmod(x, g, shift, scale, out_dtype):
    bsz, t, d = x.shape
    tt = _tile(t, 512, SUBLANES)
    row = pl.BlockSpec((1, tt, d), lambda b, i: (b, i, 0))
    return pl.pallas_call(
        _normmod_body,
        grid=(bsz, t // tt),
        in_specs=[row, pl.BlockSpec((1, d), lambda b, i: (0, 0)), _mod_spec(shift, tt), _mod_spec(scale, tt)],
        out_specs=row,
        out_shape=jax.ShapeDtypeStruct((bsz, t, d), out_dtype),
        compiler_params=_params("arbitrary", "arbitrary"),
        name="normmod",
    )(x, g.reshape(1, d), shift, scale)


def _resnormmod(x, y, gate, g, shift, scale, tokens, row0):
    bsz, t, d = x.shape
    tt = _tile(t, 512, SUBLANES)
    nt = t // tt
    assert row0 % tt == 0
    blk0 = row0 // tt
    row = pl.BlockSpec((1, tt, d), lambda b, i: (b, i, 0))
    return pl.pallas_call(
        _resnormmod_body,
        grid=(bsz, nt),
        in_specs=[row, row, _mod_spec(gate, tt), pl.BlockSpec((1, d), lambda b, i: (0, 0)),
                  _mod_spec(shift, tt), _mod_spec(scale, tt), pl.BlockSpec(memory_space=pl.ANY)],
        out_specs=[row, pl.BlockSpec((tt, d), lambda b, i: (blk0 + b * nt + i, 0))],
        out_shape=[jax.ShapeDtypeStruct((bsz, t, d), F32), jax.ShapeDtypeStruct(tokens.shape, F32)],
        input_output_aliases={6: 1},
        compiler_params=_params("arbitrary", "arbitrary"),
        name="resnormmod",
    )(x, y, gate, g.reshape(1, d), shift, scale, tokens)


def _wkv_body(r_ref, lw_ref, k_ref, v_ref, a_ref, b_ref, s0_ref, y_ref, st_ref, s_scr, *, chunk, pairs, n_chunks,
              n_levels):
    tstep = pl.program_id(2)

    hs = HEAD_SIZE

    @pl.when(tstep == 0)
    def _():
        s_scr[...] = jnp.zeros_like(s_scr)
        for p in range(pairs):
            s_scr[p, 0:hs, 0:hs] = s0_ref[0, 2 * p]
            s_scr[p, hs:2 * hs, hs:2 * hs] = s0_ref[0, 2 * p + 1]

    rows = 2 * chunk
    lane = lax.broadcasted_iota(jnp.int32, (chunk, LANES), 1)
    head0 = (lane < HEAD_SIZE).astype(F32)
    head1 = 1.0 - head0
    ri = lax.broadcasted_iota(jnp.int32, (rows, rows), 0)
    ci = lax.broadcasted_iota(jnp.int32, (rows, rows), 1)
    strict = ri > ci
    incl = ri >= ci
    eye = (ri == ci).astype(F32)
    merge = [strict & ((ri >> (q + 1)) == (ci >> (q + 1))) & ((ri >> q) != (ci >> q)) for q in range(n_levels)]
    ti = lax.broadcasted_iota(jnp.int32, (chunk, chunk), 0)
    tj = lax.broadcasted_iota(jnp.int32, (chunk, chunk), 1)
    tri = (ti >= tj).astype(F32)

    def stack2(x):
        return jnp.concatenate([x * head0, x * head1], axis=0)

    def one_chunk(c, carry):
        sl = pl.ds(pl.multiple_of(c * chunk, chunk), chunk)
        ps = range(pairs)
        ls = [slice(p * LANES, (p + 1) * LANES) for p in ps]
        lw = [lw_ref[0, sl, ls[p]] for p in ps]
        cl = [jnp.dot(tri, lw[p], precision=HIGHEST, preferred_element_type=F32) for p in ps]
        c_in = [jnp.exp(cl[p]) for p in ps]
        c_inv = [jnp.exp(-cl[p]) for p in ps]
        c_ex = [jnp.exp(cl[p] - lw[p]) for p in ps]
        rt = [stack2(r_ref[0, sl, ls[p]] * c_in[p]).astype(BF16) for p in ps]
        at = [stack2(a_ref[0, sl, ls[p]] * c_ex[p]).astype(BF16) for p in ps]
        bt = [stack2(b_ref[0, sl, ls[p]] * c_inv[p]).astype(BF16) for p in ps]
        kt = [stack2(k_ref[0, sl, ls[p]] * c_inv[p]).astype(BF16) for p in ps]
        v2 = [stack2(v_ref[0, sl, ls[p]]).astype(BF16) for p in ps]
        s = [s_scr[p] for p in ps]
        sb = [s[p].astype(BF16) for p in ps]
        l_ab = [jnp.where(strict, _bdot(at[p], bt[p], _NT), 0.0) for p in ps]
        l_ak = [jnp.where(strict, _bdot(at[p], kt[p], _NT), 0.0).astype(BF16) for p in ps]
        m_rb = [jnp.where(incl, _bdot(rt[p], bt[p], _NT), 0.0).astype(BF16) for p in ps]
        m_rk = [jnp.where(incl, _bdot(rt[p], kt[p], _NT), 0.0).astype(BF16) for p in ps]
        rhs = [_bdot(at[p], sb[p], _NT) + _bdot(l_ak[p], v2[p]) for p in ps]
        y0 = [_bdot(rt[p], sb[p], _NT) + _bdot(m_rk[p], v2[p]) for p in ps]
        tinv = [eye + jnp.where(merge[0], l_ab[p], 0.0) for p in ps]
        for lv in range(1, n_levels):
            tb = [tinv[p].astype(BF16) for p in ps]
            off = [_bdot(jnp.where(merge[lv], l_ab[p], 0.0), tb[p]) for p in ps]
            tinv = [tinv[p] + _bdot(tb[p], off[p]) for p in ps]
        u = [_bdot(tinv[p], rhs[p]).astype(BF16) for p in ps]
        y2 = [y0[p] + _bdot(m_rb[p], u[p]) for p in ps]
        upd = [_bdot(u[p], bt[p], _TN) + _bdot(v2[p], kt[p], _TN) for p in ps]
        for p in ps:
            y_ref[0, sl, ls[p]] = y2[p][:chunk] + y2[p][chunk:]
            s_scr[p] = (s[p] + upd[p]) * c_in[p][chunk - 1:chunk, :]
        return carry

    lax.fori_loop(0, n_chunks, one_chunk, 0)

    @pl.when(tstep == pl.num_programs(2) - 1)
    def _():
        for p in range(pairs):
            st_ref[0, 2 * p] = s_scr[p, 0:hs, 0:hs]
            st_ref[0, 2 * p + 1] = s_scr[p, hs:2 * hs, hs:2 * hs]


def _wkv(r, lw, k, v, a, b, s0, layer):
    bsz, t, d = r.shape
    t_real = t
    if t % SUBLANES:
        pad = SUBLANES - t % SUBLANES
        r, lw, k, v, a, b = (jnp.pad(z, ((0, 0), (0, pad), (0, 0))) for z in (r, lw, k, v, a, b))
        t += pad
    chunk = _tile(t, WKV_CHUNK, SUBLANES)
    tt = _tile(t, WKV_TIME_TILE, chunk)
    n_pairs = d // LANES
    pairs = _tile(n_pairs, WKV_PAIRS, 1)
    n_levels = (chunk - 1).bit_length()
    assert chunk == 1 << n_levels, "the block-merge inverse needs a power-of-two chunk"
    seq = pl.BlockSpec((1, tt, pairs * LANES), lambda i, p, j: (i, j, p))
    st_shape = (2 * pairs, HEAD_SIZE, HEAD_SIZE)
    y, s_t = pl.pallas_call(
        functools.partial(_wkv_body, chunk=chunk, pairs=pairs, n_chunks=tt // chunk, n_levels=n_levels),
        grid=(bsz, n_pairs // pairs, t // tt),
        in_specs=[seq] * 6 + [pl.BlockSpec((None, 1) + st_shape, lambda i, p, j: (layer, i, p, 0, 0))],
        out_specs=[seq, pl.BlockSpec((1,) + st_shape, lambda i, p, j: (i, p, 0, 0))],
        out_shape=[jax.ShapeDtypeStruct((bsz, t, d), F32),
                   jax.ShapeDtypeStruct((bsz, 2 * n_pairs, HEAD_SIZE, HEAD_SIZE), F32)],
        scratch_shapes=[pltpu.VMEM((pairs, LANES, LANES), F32)],
        compiler_params=_params("arbitrary", "arbitrary", "arbitrary"),
        name="wkv",
    )(r, lw, k, v, a, b, s0)
    return y[:, :t_real], s_t


def _dwconv_body(u_ref, buf_ref, w_ref, b_ref, lnw_ref, lnb_ref, o_ref, win_ref, *, tt, width, halo):
    @pl.when(pl.program_id(1) == 0)
    def _():
        win_ref[0:halo, :] = buf_ref[0]

    win_ref[halo:halo + tt, :] = u_ref[0]
    first = halo - (width - 1)
    z = b_ref[...] + w_ref[0:1, :] * win_ref[first:first + tt, :]
    for j in range(1, width):
        z = z + w_ref[j:j + 1, :] * win_ref[first + j:first + j + tt, :]
    mu = jnp.mean(z, axis=-1, keepdims=True)
    zc = z - mu
    var = jnp.mean(zc * zc, axis=-1, keepdims=True)
    zn = zc * lax.rsqrt(var + CONV_LN_EPS) * lnw_ref[...] + lnb_ref[...]
    o_ref[0] = (zn * jax.nn.sigmoid(zn)).astype(o_ref.dtype)
    win_ref[0:halo, :] = win_ref[tt:tt + halo, :]


def _dwconv_ln_silu(u, buf, w_dw, b_dw, ln_w, ln_b):
    bsz, t, d = u.shape
    width = w_dw.shape[0]
    halo = -(-(width - 1) // SUBLANES) * SUBLANES
    tt = _tile(t, 128, SUBLANES)
    assert tt >= halo or tt == t
    bufp = jnp.pad(buf, ((0, 0), (halo - (width - 1), 0), (0, 0)))
    vec = pl.BlockSpec((1, d), lambda i, j: (0, 0))
    return pl.pallas_call(
        functools.partial(_dwconv_body, tt=tt, width=width, halo=halo),
        grid=(bsz, t // tt),
        in_specs=[pl.BlockSpec((1, tt, d), lambda i, j: (i, j, 0)),
                  pl.BlockSpec((1, halo, d), lambda i, j: (i, 0, 0)),
                  pl.BlockSpec((width, d), lambda i, j: (0, 0)), vec, vec, vec],
        out_specs=pl.BlockSpec((1, tt, d), lambda i, j: (i, j, 0)),
        out_shape=jax.ShapeDtypeStruct((bsz, t, d), BF16),
        scratch_shapes=[pltpu.VMEM((halo + max(tt, halo), d), F32)],
        compiler_params=_params("arbitrary", "arbitrary"),
        name="dwconv_ln_silu",
    )(u, bufp, w_dw, b_dw.reshape(1, d), ln_w.reshape(1, d), ln_b.reshape(1, d))


def _pool_body(h_ref, buf_ref, w_ref, sc_ref, o_ref, win_ref, wb_ref, *, tt, halo, start_pos, group):
    tstep = pl.program_id(1)

    @pl.when((pl.program_id(0) == 0) & (tstep == 0))
    def _():
        wb_ref[...] = w_ref[...].astype(BF16)

    @pl.when(tstep == 0)
    def _():
        win_ref[0:halo, :] = buf_ref[0]

    win_ref[halo:halo + tt, :] = h_ref[0]
    pos = start_pos + tstep * tt + lax.broadcasted_iota(jnp.int32, (tt, 1), 0)
    for gi, win in enumerate(POOL_WINDOWS):
        cs = slice(gi * group, (gi + 1) * group)
        tot = win_ref[halo:halo + tt, cs]
        for back in range(1, win):
            tot = tot + win_ref[halo - back:halo - back + tt, cs]
        cnt = jnp.minimum(pos + 1, win).astype(F32)
        dlt = tot / cnt - win_ref[halo:halo + tt, cs]
        o_ref[0, :, cs] = jnp.dot(dlt.astype(BF16), wb_ref[gi], preferred_element_type=F32) * sc_ref[:, cs]
    win_ref[0:halo, :] = win_ref[tt:tt + halo, :]


def _pool_mixer(h, buf, start_pos, w_grp, scale):
    bsz, t, d = h.shape
    ng, group, _ = w_grp.shape
    halo = -(-POOL_BUF // SUBLANES) * SUBLANES
    tt = _tile(t, 256, SUBLANES)
    assert tt >= halo or tt == t
    bufp = jnp.pad(buf, ((0, 0), (halo - POOL_BUF, 0), (0, 0)))
    return pl.pallas_call(
        functools.partial(_pool_body, tt=tt, halo=halo, start_pos=start_pos, group=group),
        grid=(bsz, t // tt),
        in_specs=[pl.BlockSpec((1, tt, d), lambda i, j: (i, j, 0)),
                  pl.BlockSpec((1, halo, d), lambda i, j: (i, 0, 0)),
                  pl.BlockSpec((ng, group, group), lambda i, j: (0, 0, 0)),
                  pl.BlockSpec((1, d), lambda i, j: (0, 0))],
        out_specs=pl.BlockSpec((1, tt, d), lambda i, j: (i, j, 0)),
        out_shape=jax.ShapeDtypeStruct((bsz, t, d), F32),
        scratch_shapes=[pltpu.VMEM((halo + max(tt, halo), d), F32), pltpu.VMEM((ng, group, group), BF16)],
        compiler_params=_params("arbitrary", "arbitrary"),
        name="pool_mixer",
    )(h, bufp, w_grp, scale.reshape(1, d))


def _keep_expert_rows(meta_ref, i, new, out_ref):
    rows = lax.broadcasted_iota(jnp.int32, new.shape, 0)
    mine = (rows >= meta_ref[5, i]) & (rows < meta_ref[6, i])

    @pl.when(meta_ref[7, i] == 1)
    def _():
        out_ref[...] = jnp.where(mine, new, 0.0).astype(out_ref.dtype)

    @pl.when(meta_ref[7, i] == 0)
    def _():
        out_ref[...] = jnp.where(mine, new.astype(out_ref.dtype), out_ref[...])


def _stream_run_weights(meta_ref, cnt_ref, start_copy, stage, sem, cast):
    j, i = pl.program_id(0), pl.program_id(1)
    n_used, n_runs = cnt_ref[0], cnt_ref[1]

    @pl.when((i < n_used) & (meta_ref[1, i] == 1))
    def _():
        run = meta_ref[2, i]
        g = j * n_runs + run
        slot = lax.rem(g, 2)

        @pl.when(g == 0)
        def _():
            start_copy(meta_ref[0, 0], 0, 0)

        pltpu.make_async_copy(stage.at[slot], stage.at[slot], sem.at[slot]).wait()
        last_run = run + 1 == n_runs

        @pl.when(jnp.logical_not(last_run) | (j + 1 < pl.num_programs(0)))
        def _():
            start_copy(meta_ref[3, i], jnp.where(last_run, j + 1, j), 1 - slot)

        cast(slot)


def _moe_up_body(meta_ref, cnt_ref, x_ref, w_hbm, bg_ref, bu_ref, h_ref, stage, wb_ref, sem, *, layer, tf, f):
    def start_copy(e, tile, slot):
        col = pl.multiple_of(tile * tf, tf)
        w_e = w_hbm.at[layer, e]
        pltpu.make_async_copy(w_e.at[:, pl.ds(col, tf)], stage.at[slot, 0],
                              sem.at[slot]).start(priority=WEIGHT_DMA_PRIORITY)
        pltpu.make_async_copy(w_e.at[:, pl.ds(f + col, tf)], stage.at[slot, 1],
                              sem.at[slot]).start(priority=WEIGHT_DMA_PRIORITY)

    def cast(slot):
        wb_ref[0] = stage[slot, 0].astype(BF16)
        wb_ref[1] = stage[slot, 1].astype(BF16)

    _stream_run_weights(meta_ref, cnt_ref, start_copy, stage, sem, cast)
    i = pl.program_id(1)

    @pl.when(i < cnt_ref[0])
    def _():
        x = x_ref[...]
        g = jnp.dot(x, wb_ref[0], preferred_element_type=F32) + bg_ref[...]
        u = jnp.dot(x, wb_ref[1], preferred_element_type=F32) + bu_ref[...]
        g = jnp.minimum(g, SWIGLU_LIMIT)
        u = jnp.clip(u, -SWIGLU_LIMIT, SWIGLU_LIMIT)
        _keep_expert_rows(meta_ref, i, (u + 1.0) * g * jax.nn.sigmoid(SWIGLU_ALPHA * g), h_ref)


def _moe_down_body(meta_ref, cnt_ref, h_ref, w_hbm, b_ref, y_ref, stage, wb_ref, sem, *, layer, tn):
    def start_copy(e, tile, slot):
        col = pl.multiple_of(tile * tn, tn)
        pltpu.make_async_copy(w_hbm.at[layer, e].at[:, pl.ds(col, tn)], stage.at[slot],
                              sem.at[slot]).start(priority=WEIGHT_DMA_PRIORITY)

    def cast(slot):
        wb_ref[...] = stage[slot].astype(BF16)

    _stream_run_weights(meta_ref, cnt_ref, start_copy, stage, sem, cast)
    i = pl.program_id(1)

    @pl.when(i < cnt_ref[0])
    def _():
        y = jnp.dot(h_ref[...], wb_ref[...], preferred_element_type=F32) + b_ref[...]
        _keep_expert_rows(meta_ref, i, y, y_ref)


def _moe_items(counts, n_blocks, bm):
    n_exp = counts.shape[0]
    n_items_max = n_blocks + n_exp - 1
    end = jnp.cumsum(counts)
    start = end - counts
    first_blk = start // bm
    n_e = jnp.where(counts > 0, (end - 1) // bm - first_blk + 1, 0)
    item_end = jnp.cumsum(n_e)
    n_items = item_end[-1]
    w = jnp.arange(n_items_max, dtype=jnp.int32)
    live = w < n_items
    wc = jnp.minimum(w, n_items - 1)
    e = jnp.minimum(jnp.searchsorted(item_end, wc, side="right"), n_exp - 1).astype(jnp.int32)
    blk = first_blk[e] + wc - (item_end[e] - n_e[e])
    lo = jnp.where(live, jnp.clip(start[e] - blk * bm, 0, bm), 0)
    hi = jnp.where(live, jnp.clip(end[e] - blk * bm, 0, bm), 0)
    opens = live & ((w == 0) | (e != jnp.roll(e, 1)))
    run = jnp.cumsum(opens.astype(jnp.int32)) - 1
    starts = jnp.where(opens, w, n_items_max)
    later = jnp.concatenate([lax.cummin(starts, reverse=True)[1:], jnp.full((1,), n_items_max, jnp.int32)])
    nxt_e = e[jnp.where(later < n_items_max, later, 0)]
    first_visit = live & ((w == 0) | (blk != jnp.roll(blk, 1)))
    meta = jnp.stack([e, opens, run, nxt_e, blk, lo, hi, first_visit]).astype(jnp.int32)
    cnt = jnp.stack([n_items, jnp.sum(opens.astype(jnp.int32))]).astype(jnp.int32)
    return meta, cnt, n_items_max


def _moe_experts(xs, counts, w_gu, b_gu, w_down, b_down, layer):
    n_rows, d = xs.shape
    _, n_exp, _, f2 = w_gu.shape
    f = f2 // 2
    bm = MOE_ROWS
    meta, cnt, n_items = _moe_items(counts, n_rows // bm, bm)
    tf = _tile(f, MOE_COLS, LANES)
    nf = f // tf
    b_gu3 = b_gu[layer].reshape(n_exp, 1, f2)
    hidden = pl.pallas_call(
        functools.partial(_moe_up_body, layer=layer, tf=tf, f=f),
        grid_spec=pltpu.PrefetchScalarGridSpec(
            num_scalar_prefetch=2,
            grid=(nf, n_items),
            in_specs=[pl.BlockSpec((bm, d), lambda j, i, mt, ct: (mt[4, i], 0)),
                      pl.BlockSpec(memory_space=pl.ANY),
                      pl.BlockSpec((None, 1, tf), lambda j, i, mt, ct: (mt[0, i], 0, j)),
                      pl.BlockSpec((None, 1, tf), lambda j, i, mt, ct: (mt[0, i], 0, j + nf))],
            out_specs=pl.BlockSpec((bm, tf), lambda j, i, mt, ct: (mt[4, i], j)),
            scratch_shapes=[pltpu.VMEM((2, 2, d, tf), F32), pltpu.VMEM((2, d, tf), BF16),
                            pltpu.SemaphoreType.DMA((2,))]),
        out_shape=jax.ShapeDtypeStruct((n_rows, f), BF16),
        compiler_params=_params("arbitrary", "arbitrary"),
        name="moe_up",
    )(meta, cnt, xs, w_gu, b_gu3, b_gu3)
    tn = _tile(d, MOE_COLS, LANES)
    return pl.pallas_call(
        functools.partial(_moe_down_body, layer=layer, tn=tn),
        grid_spec=pltpu.PrefetchScalarGridSpec(
            num_scalar_prefetch=2,
            grid=(d // tn, n_items),
            in_specs=[pl.BlockSpec((bm, f), lambda j, i, mt, ct: (mt[4, i], 0)),
                      pl.BlockSpec(memory_space=pl.ANY),
                      pl.BlockSpec((None, 1, tn), lambda j, i, mt, ct: (mt[0, i], 0, j))],
            out_specs=pl.BlockSpec((bm, tn), lambda j, i, mt, ct: (mt[4, i], j)),
            scratch_shapes=[pltpu.VMEM((2, f, tn), F32), pltpu.VMEM((f, tn), BF16),
                            pltpu.SemaphoreType.DMA((2,))]),
        out_shape=jax.ShapeDtypeStruct((n_rows, d), F32),
        compiler_params=_params("arbitrary", "arbitrary"),
        name="moe_down",
    )(meta, cnt, hidden, w_down, b_down[layer].reshape(n_exp, 1, d))


def _dispatch_body(idx_ref, nxt_ref, x_hbm, o_ref, buf, sem, *, bm):
    step = pl.program_id(0)
    slot = lax.rem(step, 2)

    def start_block(ref, dst_slot):
        def body(r2, carry):
            for q in range(2):
                r = 2 * r2 + q
                pltpu.make_async_copy(x_hbm.at[pl.ds(ref[0, r], 1)], buf.at[dst_slot, pl.ds(r, 1)],
                                      sem.at[dst_slot]).start(priority=q)
            return carry
        lax.fori_loop(0, bm // 2, body, 0, unroll=8)

    @pl.when(step == 0)
    def _():
        start_block(idx_ref, 0)

    @pl.when(step + 1 < pl.num_programs(0))
    def _():
        start_block(nxt_ref, 1 - slot)

    pltpu.make_async_copy(buf.at[slot], buf.at[slot], sem.at[slot]).wait()
    o_ref[...] = buf[slot].astype(o_ref.dtype)


def _dispatch(tokens, row_tok):
    n_rows = row_tok.shape[0]
    d = tokens.shape[1]
    bm = MOE_ROWS
    n_blocks = n_rows // bm
    idx3 = row_tok.astype(jnp.int32).reshape(n_blocks, 1, bm)
    return pl.pallas_call(
        functools.partial(_dispatch_body, bm=bm),
        grid=(n_blocks,),
        in_specs=[pl.BlockSpec((None, 1, bm), lambda i: (i, 0, 0), memory_space=pltpu.SMEM),
                  pl.BlockSpec((None, 1, bm), lambda i: (jnp.minimum(i + 1, n_blocks - 1), 0, 0),
                               memory_space=pltpu.SMEM),
                  pl.BlockSpec(memory_space=pl.ANY)],
        out_specs=pl.BlockSpec((bm, d), lambda i: (i, 0)),
        out_shape=jax.ShapeDtypeStruct((n_rows, d), BF16),
        scratch_shapes=[pltpu.VMEM((2, bm, d), F32), pltpu.SemaphoreType.DMA((2,))],
        compiler_params=_params("arbitrary"),
        name="moe_dispatch",
    )(idx3, idx3, tokens)


def _moe(hf, w_router, b_router, w_gu, b_gu, w_down, b_down, layer):
    n, d = hf.shape
    n_exp = w_router.shape[2]
    bm = MOE_ROWS
    logits = _matmul(hf, w_router, b_router, layer, exact=True)
    top_v, top_e = lax.top_k(logits, TOP_K)
    gates = jax.nn.softmax(top_v, axis=-1)
    n_assign = n * TOP_K
    flat_e = top_e.reshape(n_assign).astype(jnp.int32)
    counts = jnp.sum((flat_e[:, None] == jnp.arange(n_exp, dtype=jnp.int32)[None, :]).astype(jnp.int32), axis=0)
    ids = jnp.arange(n_assign, dtype=jnp.int32)
    _, order = lax.sort((flat_e, ids), num_keys=1, is_stable=True)
    _, pos = lax.sort((order, ids), num_keys=1)
    n_rows = -(-n_assign // bm) * bm
    row_tok = jnp.pad(order // TOP_K, (0, n_rows - n_assign))
    y = _moe_experts(_dispatch(hf, row_tok), counts, w_gu, b_gu, w_down, b_down, layer)
    return y, pos.reshape(n, TOP_K), gates


def _combine_body(pos_ref, nxt_ref, y_hbm, eg_ref, x_ref, gt_ref, g_ref, *rest, tm, final):
    if final:
        o_ref, buf, sem = rest
    else:
        sh_ref, sc_ref, xo_ref, h_ref, buf, sem = rest
    step = pl.program_id(0) * pl.num_programs(1) + pl.program_id(1)
    n_steps = pl.num_programs(0) * pl.num_programs(1)
    slot = lax.rem(step, 2)

    def start_block(idx_ref, dst_slot):
        def body(r, carry):
            for k in range(TOP_K):
                row = idx_ref[0, r * TOP_K + k]
                pltpu.make_async_copy(y_hbm.at[pl.ds(row, 1)], buf.at[dst_slot, k, pl.ds(r, 1)],
                                      sem.at[dst_slot]).start(priority=k % 2)
            return carry
        lax.fori_loop(0, tm, body, 0, unroll=8)

    @pl.when(step == 0)
    def _():
        start_block(pos_ref, 0)

    @pl.when(step + 1 < n_steps)
    def _():
        start_block(nxt_ref, 1 - slot)

    pltpu.make_async_copy(buf.at[slot], buf.at[slot], sem.at[slot]).wait()
    eg = eg_ref[0]
    moe = eg[:, 0:1] * buf[slot, 0]
    for k in range(1, TOP_K):
        moe = moe + eg[:, k:k + 1] * buf[slot, k]
    x = x_ref[0] + gt_ref[0] * moe
    if final:
        o_ref[0] = (x * lax.rsqrt(jnp.mean(x * x, axis=-1, keepdims=True) + RMS_EPS)) * g_ref[...]
    else:
        xo_ref[0] = x
        h_ref[0] = _rms_mod(x, g_ref[...], sh_ref[0], sc_ref[0]).astype(h_ref.dtype)


def _moe_combine(y, pos, expert_gates, x, gate, g, shift=None, scale=None):
    bsz, t, d = x.shape
    final = shift is None
    tm = _tile(t, 128, SUBLANES)
    nt = t // tm
    n_steps = bsz * nt
    pos3 = pos.astype(jnp.int32).reshape(n_steps, 1, tm * TOP_K)
    idx = pl.BlockSpec((None, 1, tm * TOP_K), lambda i, j: (i * nt + j, 0, 0), memory_space=pltpu.SMEM)
    nxt = pl.BlockSpec((None, 1, tm * TOP_K), lambda i, j: (jnp.minimum(i * nt + j + 1, n_steps - 1), 0, 0),
                       memory_space=pltpu.SMEM)
    row = pl.BlockSpec((1, tm, d), lambda i, j: (i, j, 0))
    vec = pl.BlockSpec((1, d), lambda i, j: (0, 0))
    in_specs = [idx, nxt, pl.BlockSpec(memory_space=pl.ANY), pl.BlockSpec((1, tm, TOP_K), lambda i, j: (i, j, 0)),
                row, _mod_spec(gate, tm), vec]
    args = [pos3, pos3, y, expert_gates.reshape(bsz, t, TOP_K), x, gate, g.reshape(1, d)]
    if final:
        out_specs, out_shape = row, jax.ShapeDtypeStruct((bsz, t, d), F32)
    else:
        in_specs += [_mod_spec(shift, tm), _mod_spec(scale, tm)]
        args += [shift, scale]
        out_specs = [row, row]
        out_shape = [jax.ShapeDtypeStruct((bsz, t, d), F32), jax.ShapeDtypeStruct((bsz, t, d), F32)]
    return pl.pallas_call(
        functools.partial(_combine_body, tm=tm, final=final),
        grid=(bsz, nt),
        in_specs=in_specs,
        out_specs=out_specs,
        out_shape=out_shape,
        scratch_shapes=[pltpu.VMEM((2, TOP_K, tm, d), F32), pltpu.SemaphoreType.DMA((2,))],
        compiler_params=_params("arbitrary", "arbitrary"),
        name="moe_combine",
    )(*args)


def _rwkv7(h, shift_prev, s0, v_first, p, j):
    bsz, t, d = h.shape
    n = bsz * t
    heads = d // HEAD_SIZE
    prev = jnp.concatenate([shift_prev[:, None], h[:, :-1]], axis=1)
    xx = prev - h
    mix = p["rwkv_mix"][j]
    xr, xw, xk, xv, xa, xg = ((h + xx * mix[m]).astype(BF16).reshape(n, d) for m in range(6))
    r = _matmul(xr, p["rwkv_wr"], layer=j)
    k = _matmul(xk, p["rwkv_wk"], layer=j)
    v = _matmul(xv, p["rwkv_wv"], layer=j)
    if j == 0:
        v_first = v
    else:
        v_lora = _matmul(_matmul(xv, p["rwkv_v1"], layer=j - 1, out_dtype=BF16), p["rwkv_v2"], layer=j - 1)
        v = v + (v_first - v) * jax.nn.sigmoid(p["rwkv_v0"][j - 1] + v_lora)
    w_lora = _matmul(_matmul(xw, p["rwkv_w1"], layer=j, act="tanh", out_dtype=BF16), p["rwkv_w2"], layer=j)
    logw = -jax.nn.softplus(-(p["rwkv_w0"][j] + w_lora)) - 0.5
    log_decay = -jnp.exp(logw)
    a_lora = _matmul(_matmul(xa, p["rwkv_a1"], layer=j, out_dtype=BF16), p["rwkv_a2"], layer=j)
    a = jax.nn.sigmoid(p["rwkv_a0"][j] + a_lora)
    g = _matmul(_matmul(xg, p["rwkv_g1"], layer=j, act="sigmoid", out_dtype=BF16), p["rwkv_g2"], layer=j)
    kk = (k * p["rwkv_kk"][j]).reshape(n, heads, HEAD_SIZE)
    kk = kk / jnp.maximum(jnp.sqrt(jnp.sum(kk * kk, axis=-1, keepdims=True)), 1e-12)
    kk = kk.reshape(n, d)
    k = k * (1.0 + (a - 1.0) * p["rwkv_ka"][j])
    seq = lambda z: z.reshape(bsz, t, d)
    y, s_t = _wkv(seq(r), seq(log_decay), seq(k), seq(v), seq(-kk), seq(kk * a), s0, j)
    y = y.reshape(n, heads, HEAD_SIZE)
    mu = jnp.mean(y, axis=-1, keepdims=True)
    var = jnp.mean(jnp.square(y - mu), axis=-1, keepdims=True)
    y = ((y - mu) * lax.rsqrt(var + LN_X_EPS)).reshape(n, d) * p["rwkv_lnx_w"][j] + p["rwkv_lnx_b"][j]
    rk = (r * k).reshape(n, heads, HEAD_SIZE) * p["rwkv_rk"][j]
    bonus = (jnp.sum(rk, axis=-1, keepdims=True) * v.reshape(n, heads, HEAD_SIZE)).reshape(n, d)
    out = _matmul(((y + bonus) * g).astype(BF16), p["rwkv_wo"], layer=j)
    return out.reshape(bsz, t, d), h[:, -1], s_t, v_first


def _conformer_conv(h, buf, p, j):
    bsz, t, d = h.shape
    width = p["conv_w_dw"].shape[1]
    u = _glu_matmul(h.reshape(bsz * t, d), p["conv_w_in"], p["conv_b_in"], j).reshape(bsz, t, -1)
    z = _dwconv_ln_silu(u, buf, p["conv_w_dw"][j], p["conv_b_dw"][j], p["conv_ln_w"][j], p["conv_ln_b"][j])
    out = _matmul(z.reshape(bsz * t, -1), p["conv_w_out"], p["conv_b_out"], j).reshape(bsz, t, d)
    new_buf = jnp.concatenate([buf, u], axis=1)[:, -(width - 1):]
    return out, new_buf


def _pool(h, buf, start_pos, p, j):
    out = _pool_mixer(h, buf, start_pos, p["pool_w"][j], p["pool_scale"][j])
    return out, jnp.concatenate([buf, h], axis=1)[:, -POOL_BUF:]


def kernel(x_prompt, x_sample, c_prompt, c_sample, state_wkv, state_shift, state_conv, state_pool, ada_w, ada_b, norm_mix, norm_ffn, final_norm, rwkv_mix, rwkv_wr, rwkv_wk, rwkv_wv, rwkv_wo, rwkv_w0, rwkv_w1, rwkv_w2, rwkv_a0, rwkv_a1, rwkv_a2, rwkv_v0, rwkv_v1, rwkv_v2, rwkv_g1, rwkv_g2, rwkv_kk, rwkv_ka, rwkv_rk, rwkv_lnx_w, rwkv_lnx_b, conv_w_in, conv_b_in, conv_w_dw, conv_b_dw, conv_ln_w, conv_ln_b, conv_w_out, conv_b_out, pool_w, pool_scale, moe_w_router, moe_b_router, moe_w_gu, moe_b_gu, moe_w_down, moe_b_down):
    p = dict(rwkv_mix=rwkv_mix, rwkv_wr=rwkv_wr, rwkv_wk=rwkv_wk, rwkv_wv=rwkv_wv, rwkv_wo=rwkv_wo,
             rwkv_w0=rwkv_w0, rwkv_w1=rwkv_w1, rwkv_w2=rwkv_w2, rwkv_a0=rwkv_a0, rwkv_a1=rwkv_a1,
             rwkv_a2=rwkv_a2, rwkv_v0=rwkv_v0, rwkv_v1=rwkv_v1, rwkv_v2=rwkv_v2, rwkv_g1=rwkv_g1,
             rwkv_g2=rwkv_g2, rwkv_kk=rwkv_kk, rwkv_ka=rwkv_ka, rwkv_rk=rwkv_rk, rwkv_lnx_w=rwkv_lnx_w,
             rwkv_lnx_b=rwkv_lnx_b, conv_w_in=conv_w_in, conv_b_in=conv_b_in, conv_w_dw=conv_w_dw,
             conv_b_dw=conv_b_dw, conv_ln_w=conv_ln_w, conv_ln_b=conv_ln_b, conv_w_out=conv_w_out,
             conv_b_out=conv_b_out, pool_w=pool_w, pool_scale=pool_scale)
    depth = ada_w.shape[0]
    bp, tp, d = x_prompt.shape
    bs, ts, _ = x_sample.shape
    heads = d // HEAD_SIZE
    n_p, n_s = bp * tp, bs * ts
    width = conv_w_dw.shape[1]

    c_act = jax.nn.silu(jnp.concatenate([c_prompt, c_sample], axis=0))
    mods = []
    for i in range(depth):
        m = _matmul(c_act, ada_w, ada_b, i)
        six = jnp.split(m, 6, axis=-1)
        mods.append(([z[:bp, None, :] for z in six],
                     [jnp.repeat(z[bp:], ts, axis=0)[None] for z in six]))

    groups = [
        dict(x=x_prompt, b=bp, t=tp, start=0,
             wkv=jnp.zeros((state_wkv.shape[0], bp, heads, HEAD_SIZE, HEAD_SIZE), F32),
             shift=jnp.zeros((state_shift.shape[0], bp, d), F32),
             conv=jnp.zeros((state_conv.shape[0], bp, width - 1, conv_w_dw.shape[2]), F32),
             pool=jnp.zeros((state_pool.shape[0], bp, POOL_BUF, d), F32)),
        dict(x=x_sample.reshape(1, n_s, d), b=bs, t=ts, start=PAST_LEN,
             wkv=state_wkv, shift=state_shift, conv=state_conv, pool=state_pool),
    ]
    for gi, grp in enumerate(groups):
        grp.update(wkv_out=[], shift_out=[], conv_out=[], pool_out=[], v_first=None)
        sh1, sc1 = mods[0][gi][0], mods[0][gi][1]
        grp["h"] = _normmod(grp["x"], norm_mix[0], sh1, sc1, F32)

    for i in range(depth):
        kind = i % 3
        tokens = jnp.zeros((n_p + n_s, d), F32)
        for gi, grp in enumerate(groups):
            sh1, sc1, gt1, sh2, sc2, gt2 = mods[i][gi]
            h = grp["h"].reshape(grp["b"], grp["t"], d)
            if kind == 0:
                j = len(grp["wkv_out"])
                out, last, s_t, grp["v_first"] = _rwkv7(h, grp["shift"][j], grp["wkv"], grp["v_first"], p, j)
                grp["wkv_out"].append(s_t)
                grp["shift_out"].append(last)
            elif kind == 1:
                j = len(grp["conv_out"])
                out, buf = _conformer_conv(h, grp["conv"][j], p, j)
                grp["conv_out"].append(buf)
            else:
                j = len(grp["pool_out"])
                out, buf = _pool(h, grp["pool"][j], grp["start"], p, j)
                grp["pool_out"].append(buf)
            grp["x"], tokens = _resnormmod(grp["x"], out.reshape(grp["x"].shape), gt1, norm_ffn[i], sh2, sc2,
                                           tokens, gi * n_p)
        y, pos, egates = _moe(tokens, moe_w_router, moe_b_router, moe_w_gu, moe_b_gu, moe_w_down, moe_b_down, i)
        for gi, grp in enumerate(groups):
            gt2 = mods[i][gi][5]
            part = slice(0, n_p) if gi == 0 else slice(n_p, None)
            if i + 1 < depth:
                sh1, sc1 = mods[i + 1][gi][0], mods[i + 1][gi][1]
                grp["x"], grp["h"] = _moe_combine(y, pos[part], egates[part], grp["x"], gt2, norm_mix[i + 1],
                                                  sh1, sc1)
            else:
                grp["x"] = _moe_combine(y, pos[part], egates[part], grp["x"], gt2, final_norm)

    gp, gs = groups
    return (gp["x"], gs["x"].reshape(bs, ts, d),
            jnp.stack(gp["wkv_out"]), jnp.stack(gp["shift_out"]), jnp.stack(gp["conv_out"]), jnp.stack(gp["pool_out"]),
            jnp.stack(gs["wkv_out"]), jnp.stack(gs["shift_out"]), jnp.stack(gs["conv_out"]), jnp.stack(gs["pool_out"]))
```

```python
import functools

import jax
import jax.numpy as jnp
from jax import lax
from jax.experimental import pallas as pl
from jax.experimental.pallas import tpu as pltpu

F32 = jnp.float32
BF16 = jnp.bfloat16
HIGHEST = lax.Precision.HIGHEST

HEAD_SIZE = 64
TOP_K = 4
POOL_WINDOWS = (2, 4, 8, 16)
POOL_BUF = max(POOL_WINDOWS) - 1
PAST_LEN = 16384
RMS_EPS = 1e-5
LN_X_EPS = 64e-5
CONV_LN_EPS = 1e-5
SWIGLU_LIMIT = 7.0
SWIGLU_ALPHA = 1.702

VMEM_LIMIT_BYTES = 56 * 1024 * 1024
SUBLANES = 8
LANES = 128
WKV_CHUNK = 64
WKV_PAIRS = 16
WKV_TIME_TILE = 128
MOE_ROWS = 256
MOE_COLS = 1024
WEIGHT_DMA_PRIORITY = 1


def _tile(n, pref, mult):
    if n <= pref:
        return n
    t = pref - pref % mult
    while t >= mult:
        if n % t == 0:
            return t
        t -= mult
    return n


def _params(*sem):
    return pltpu.CompilerParams(dimension_semantics=sem, vmem_limit_bytes=VMEM_LIMIT_BYTES)


def _bdot(a, b, dims=(((1,), (0,)), ((), ()))):
    return lax.dot_general(a.astype(BF16), b.astype(BF16), dims, preferred_element_type=F32)


_NT = (((1,), (1,)), ((), ()))
_TN = (((0,), (0,)), ((), ()))


def _apply_act(acc, act):
    if act == "tanh":
        return jnp.tanh(acc)
    if act == "sigmoid":
        return jax.nn.sigmoid(acc)
    assert act is None, act
    return acc


def _mm_body(x_ref, w_ref, b_ref, o_ref, wb_ref, *, act, exact):
    @pl.when(pl.program_id(1) == 0)
    def _():
        wb_ref[...] = w_ref[...].astype(wb_ref.dtype)

    if exact:
        acc = jnp.dot(x_ref[...].astype(F32), wb_ref[...], precision=HIGHEST, preferred_element_type=F32)
    else:
        acc = jnp.dot(x_ref[...].astype(BF16), wb_ref[...], preferred_element_type=F32)
    o_ref[...] = _apply_act(acc + b_ref[...], act).astype(o_ref.dtype)


def _matmul(x, w, b=None, layer=0, act=None, out_dtype=F32, exact=False, tm=512, tn=1024):
    m, k = x.shape
    n = w.shape[2]
    tm = _tile(m, tm, SUBLANES)
    tn = _tile(n, tn, LANES)
    b = jnp.zeros((1, n), F32) if b is None else b[layer].reshape(1, n)
    return pl.pallas_call(
        functools.partial(_mm_body, act=act, exact=exact),
        grid=(n // tn, m // tm),
        in_specs=[pl.BlockSpec((tm, k), lambda j, i: (i, 0)),
                  pl.BlockSpec((None, k, tn), lambda j, i: (layer, 0, j)),
                  pl.BlockSpec((1, tn), lambda j, i: (0, j))],
        out_specs=pl.BlockSpec((tm, tn), lambda j, i: (i, j)),
        out_shape=jax.ShapeDtypeStruct((m, n), out_dtype),
        scratch_shapes=[pltpu.VMEM((k, tn), F32 if exact else BF16)],
        compiler_params=_params("arbitrary", "arbitrary"),
        name="matmul",
    )(x, w, b)


def _glu_body(x_ref, wa_ref, wg_ref, ba_ref, bg_ref, o_ref, wab_ref, wgb_ref):
    @pl.when(pl.program_id(1) == 0)
    def _():
        wab_ref[...] = wa_ref[...].astype(BF16)
        wgb_ref[...] = wg_ref[...].astype(BF16)

    x = x_ref[...].astype(BF16)
    a = jnp.dot(x, wab_ref[...], preferred_element_type=F32) + ba_ref[...]
    g = jnp.dot(x, wgb_ref[...], preferred_element_type=F32) + bg_ref[...]
    o_ref[...] = (a * jax.nn.sigmoid(g)).astype(o_ref.dtype)


def _glu_matmul(x, w, b, layer, tm=512, tn=512):
    m, k = x.shape
    n = w.shape[2] // 2
    tm = _tile(m, tm, SUBLANES)
    tn = _tile(n, tn, LANES)
    nj = n // tn
    b = b[layer].reshape(1, 2 * n)
    return pl.pallas_call(
        _glu_body,
        grid=(nj, m // tm),
        in_specs=[pl.BlockSpec((tm, k), lambda j, i: (i, 0)),
                  pl.BlockSpec((None, k, tn), lambda j, i: (layer, 0, j)),
                  pl.BlockSpec((None, k, tn), lambda j, i: (layer, 0, j + nj)),
                  pl.BlockSpec((1, tn), lambda j, i: (0, j)),
                  pl.BlockSpec((1, tn), lambda j, i: (0, j + nj))],
        out_specs=pl.BlockSpec((tm, tn), lambda j, i: (i, j)),
        out_shape=jax.ShapeDtypeStruct((m, n), F32),
        scratch_shapes=[pltpu.VMEM((k, tn), BF16), pltpu.VMEM((k, tn), BF16)],
        compiler_params=_params("arbitrary", "arbitrary"),
        name="glu_matmul",
    )(x, w, w, b, b)


def _rms_mod(x, g, shift, scale):
    y = x * lax.rsqrt(jnp.mean(x * x, axis=-1, keepdims=True) + RMS_EPS)
    return (y * g) * (1.0 + scale) + shift


def _normmod_body(x_ref, g_ref, sh_ref, sc_ref, h_ref):
    h_ref[0] = _rms_mod(x_ref[0], g_ref[...], sh_ref[0], sc_ref[0]).astype(h_ref.dtype)


def _resnormmod_body(x_ref, y_ref, gt_ref, g_ref, sh_ref, sc_ref, tokens_hbm, xo_ref, h_ref):
    del tokens_hbm
    x = x_ref[0] + gt_ref[0] * y_ref[0]
    xo_ref[0] = x
    h_ref[...] = _rms_mod(x, g_ref[...], sh_ref[0], sc_ref[0])


def _mod_spec(mod, tt):
    if mod.shape[1] == 1:
        return pl.BlockSpec((1, 1, mod.shape[2]), lambda b, t: (b, 0, 0))
    return pl.BlockSpec((1, tt, mod.shape[2]), lambda b, t: (b, t, 0))


def _normmod(x, g, shift, scale, out_dtype):
    bsz, t, d = x.shape
    tt = _tile(t, 512, SUBLANES)
    row = pl.BlockSpec((1, tt, d), lambda b, i: (b, i, 0))
    return pl.pallas_call(
        _normmod_body,
        grid=(bsz, t // tt),
        in_specs=[row, pl.BlockSpec((1, d), lambda b, i: (0, 0)), _mod_spec(shift, tt), _mod_spec(scale, tt)],
        out_specs=row,
        out_shape=jax.ShapeDtypeStruct((bsz, t, d), out_dtype),
        compiler_params=_params("arbitrary", "arbitrary"),
        name="normmod",
    )(x, g.reshape(1, d), shift, scale)


def _resnormmod(x, y, gate, g, shift, scale, tokens, row0):
    bsz, t, d = x.shape
    tt = _tile(t, 512, SUBLANES)
    nt = t // tt
    assert row0 % tt == 0
    blk0 = row0 // tt
    row = pl.BlockSpec((1, tt, d), lambda b, i: (b, i, 0))
    return pl.pallas_call(
        _resnormmod_body,
        grid=(bsz, nt),
        in_specs=[row, row, _mod_spec(gate, tt), pl.BlockSpec((1, d), lambda b, i: (0, 0)),
                  _mod_spec(shift, tt), _mod_spec(scale, tt), pl.BlockSpec(memory_space=pl.ANY)],
        out_specs=[row, pl.BlockSpec((tt, d), lambda b, i: (blk0 + b * nt + i, 0))],
        out_shape=[jax.ShapeDtypeStruct((bsz, t, d), F32), jax.ShapeDtypeStruct(tokens.shape, F32)],
        input_output_aliases={6: 1},
        compiler_params=_params("arbitrary", "arbitrary"),
        name="resnormmod",
    )(x, y, gate, g.reshape(1, d), shift, scale, tokens)


def _wkv_body(r_ref, lw_ref, k_ref, v_ref, a_ref, b_ref, s0_ref, y_ref, st_ref, s_scr, *, chunk, pairs, n_chunks,
              n_levels):
    tstep = pl.program_id(2)

    hs = HEAD_SIZE

    @pl.when(tstep == 0)
    def _():
        s_scr[...] = jnp.zeros_like(s_scr)
        for p in range(pairs):
            s_scr[p, 0:hs, 0:hs] = s0_ref[0, 2 * p]
            s_scr[p, hs:2 * hs, hs:2 * hs] = s0_ref[0, 2 * p + 1]

    rows = 2 * chunk
    lane = lax.broadcasted_iota(jnp.int32, (chunk, LANES), 1)
    head0 = (lane < HEAD_SIZE).astype(F32)
    head1 = 1.0 - head0
    ri = lax.broadcasted_iota(jnp.int32, (rows, rows), 0)
    ci = lax.broadcasted_iota(jnp.int32, (rows, rows), 1)
    strict = ri > ci
    incl = ri >= ci
    eye = (ri == ci).astype(F32)
    merge = [strict & ((ri >> (q + 1)) == (ci >> (q + 1))) & ((ri >> q) != (ci >> q)) for q in range(n_levels)]
    ti = lax.broadcasted_iota(jnp.int32, (chunk, chunk), 0)
    tj = lax.broadcasted_iota(jnp.int32, (chunk, chunk), 1)
    tri = (ti >= tj).astype(F32)

    def stack2(x):
        return jnp.concatenate([x * head0, x * head1], axis=0)

    def one_chunk(c, carry):
        sl = pl.ds(pl.multiple_of(c * chunk, chunk), chunk)
        ps = range(pairs)
        ls = [slice(p * LANES, (p + 1) * LANES) for p in ps]
        lw = [lw_ref[0, sl, ls[p]] for p in ps]
        cl = [jnp.dot(tri, lw[p], precision=HIGHEST, preferred_element_type=F32) for p in ps]
        c_in = [jnp.exp(cl[p]) for p in ps]
        c_inv = [jnp.exp(-cl[p]) for p in ps]
        c_ex = [jnp.exp(cl[p] - lw[p]) for p in ps]
        rt = [stack2(r_ref[0, sl, ls[p]] * c_in[p]).astype(BF16) for p in ps]
        at = [stack2(a_ref[0, sl, ls[p]] * c_ex[p]).astype(BF16) for p in ps]
        bt = [stack2(b_ref[0, sl, ls[p]] * c_inv[p]).astype(BF16) for p in ps]
        kt = [stack2(k_ref[0, sl, ls[p]] * c_inv[p]).astype(BF16) for p in ps]
        v2 = [stack2(v_ref[0, sl, ls[p]]).astype(BF16) for p in ps]
        s = [s_scr[p] for p in ps]
        sb = [s[p].astype(BF16) for p in ps]
        l_ab = [jnp.where(strict, _bdot(at[p], bt[p], _NT), 0.0) for p in ps]
        l_ak = [jnp.where(strict, _bdot(at[p], kt[p], _NT), 0.0).astype(BF16) for p in ps]
        m_rb = [jnp.where(incl, _bdot(rt[p], bt[p], _NT), 0.0).astype(BF16) for p in ps]
        m_rk = [jnp.where(incl, _bdot(rt[p], kt[p], _NT), 0.0).astype(BF16) for p in ps]
        rhs = [_bdot(at[p], sb[p], _NT) + _bdot(l_ak[p], v2[p]) for p in ps]
        y0 = [_bdot(rt[p], sb[p], _NT) + _bdot(m_rk[p], v2[p]) for p in ps]
        tinv = [eye + jnp.where(merge[0], l_ab[p], 0.0) for p in ps]
        for lv in range(1, n_levels):
            tb = [tinv[p].astype(BF16) for p in ps]
            off = [_bdot(jnp.where(merge[lv], l_ab[p], 0.0), tb[p]) for p in ps]
            tinv = [tinv[p] + _bdot(tb[p], off[p]) for p in ps]
        u = [_bdot(tinv[p], rhs[p]).astype(BF16) for p in ps]
        y2 = [y0[p] + _bdot(m_rb[p], u[p]) for p in ps]
        upd = [_bdot(u[p], bt[p], _TN) + _bdot(v2[p], kt[p], _TN) for p in ps]
        for p in ps:
            y_ref[0, sl, ls[p]] = y2[p][:chunk] + y2[p][chunk:]
            s_scr[p] = (s[p] + upd[p]) * c_in[p][chunk - 1:chunk, :]
        return carry

    lax.fori_loop(0, n_chunks, one_chunk, 0)

    @pl.when(tstep == pl.num_programs(2) - 1)
    def _():
        for p in range(pairs):
            st_ref[0, 2 * p] = s_scr[p, 0:hs, 0:hs]
            st_ref[0, 2 * p + 1] = s_scr[p, hs:2 * hs, hs:2 * hs]


def _wkv(r, lw, k, v, a, b, s0, layer):
    bsz, t, d = r.shape
    t_real = t
    if t % SUBLANES:
        pad = SUBLANES - t % SUBLANES
        r, lw, k, v, a, b = (jnp.pad(z, ((0, 0), (0, pad), (0, 0))) for z in (r, lw, k, v, a, b))
        t += pad
    chunk = _tile(t, WKV_CHUNK, SUBLANES)
    tt = _tile(t, WKV_TIME_TILE, chunk)
    n_pairs = d // LANES
    pairs = _tile(n_pairs, WKV_PAIRS, 1)
    n_levels = (chunk - 1).bit_length()
    assert chunk == 1 << n_levels, "the block-merge inverse needs a power-of-two chunk"
    seq = pl.BlockSpec((1, tt, pairs * LANES), lambda i, p, j: (i, j, p))
    st_shape = (2 * pairs, HEAD_SIZE, HEAD_SIZE)
    y, s_t = pl.pallas_call(
        functools.partial(_wkv_body, chunk=chunk, pairs=pairs, n_chunks=tt // chunk, n_levels=n_levels),
        grid=(bsz, n_pairs // pairs, t // tt),
        in_specs=[seq] * 6 + [pl.BlockSpec((None, 1) + st_shape, lambda i, p, j: (layer, i, p, 0, 0))],
        out_specs=[seq, pl.BlockSpec((1,) + st_shape, lambda i, p, j: (i, p, 0, 0))],
        out_shape=[jax.ShapeDtypeStruct((bsz, t, d), F32),
                   jax.ShapeDtypeStruct((bsz, 2 * n_pairs, HEAD_SIZE, HEAD_SIZE), F32)],
        scratch_shapes=[pltpu.VMEM((pairs, LANES, LANES), F32)],
        compiler_params=_params("arbitrary", "arbitrary", "arbitrary"),
        name="wkv",
    )(r, lw, k, v, a, b, s0)
    return y[:, :t_real], s_t


def _dwconv_body(u_ref, buf_ref, w_ref, b_ref, lnw_ref, lnb_ref, o_ref, win_ref, z_ref, *, tt, width, halo, cw):
    @pl.when(pl.program_id(1) == 0)
    def _():
        win_ref[0:halo, :] = buf_ref[0]

    win_ref[halo:halo + tt, :] = u_ref[0]
    first = halo - (width - 1)

    def conv_cols(c, carry):
        cs = pl.ds(pl.multiple_of(c * cw, cw), cw)
        acc = b_ref[:, cs] + w_ref[0:1, cs] * win_ref[first:first + tt, cs]
        for j in range(1, width):
            acc = acc + w_ref[j:j + 1, cs] * win_ref[first + j:first + j + tt, cs]
        z_ref[:, cs] = acc
        return carry

    lax.fori_loop(0, z_ref.shape[1] // cw, conv_cols, 0)
    z = z_ref[...]
    mu = jnp.mean(z, axis=-1, keepdims=True)
    zc = z - mu
    var = jnp.mean(zc * zc, axis=-1, keepdims=True)
    zn = zc * lax.rsqrt(var + CONV_LN_EPS) * lnw_ref[...] + lnb_ref[...]
    o_ref[0] = (zn * jax.nn.sigmoid(zn)).astype(o_ref.dtype)
    win_ref[0:halo, :] = win_ref[tt:tt + halo, :]


def _dwconv_ln_silu(u, buf, w_dw, b_dw, ln_w, ln_b):
    bsz, t, d = u.shape
    width = w_dw.shape[0]
    halo = -(-(width - 1) // SUBLANES) * SUBLANES
    tt = _tile(t, 128, SUBLANES)
    assert tt >= halo or tt == t
    bufp = jnp.pad(buf, ((0, 0), (halo - (width - 1), 0), (0, 0)))
    vec = pl.BlockSpec((1, d), lambda i, j: (0, 0))
    cw = _tile(d, 2 * LANES, LANES)
    return pl.pallas_call(
        functools.partial(_dwconv_body, tt=tt, width=width, halo=halo, cw=cw),
        grid=(bsz, t // tt),
        in_specs=[pl.BlockSpec((1, tt, d), lambda i, j: (i, j, 0)),
                  pl.BlockSpec((1, halo, d), lambda i, j: (i, 0, 0)),
                  pl.BlockSpec((width, d), lambda i, j: (0, 0)), vec, vec, vec],
        out_specs=pl.BlockSpec((1, tt, d), lambda i, j: (i, j, 0)),
        out_shape=jax.ShapeDtypeStruct((bsz, t, d), BF16),
        scratch_shapes=[pltpu.VMEM((halo + max(tt, halo), d), F32), pltpu.VMEM((tt, d), F32)],
        compiler_params=_params("arbitrary", "arbitrary"),
        name="dwconv_ln_silu",
    )(u, bufp, w_dw, b_dw.reshape(1, d), ln_w.reshape(1, d), ln_b.reshape(1, d))


def _pool_body(h_ref, buf_ref, w_ref, sc_ref, o_ref, win_ref, wb_ref, *, tt, halo, start_pos, group):
    tstep = pl.program_id(1)

    @pl.when((pl.program_id(0) == 0) & (tstep == 0))
    def _():
        wb_ref[...] = w_ref[...].astype(BF16)

    @pl.when(tstep == 0)
    def _():
        win_ref[0:halo, :] = buf_ref[0]

    win_ref[halo:halo + tt, :] = h_ref[0]
    pos = start_pos + tstep * tt + lax.broadcasted_iota(jnp.int32, (tt, 1), 0)
    for gi, win in enumerate(POOL_WINDOWS):
        cs = slice(gi * group, (gi + 1) * group)
        tot = win_ref[halo:halo + tt, cs]
        for back in range(1, win):
            tot = tot + win_ref[halo - back:halo - back + tt, cs]
        cnt = jnp.minimum(pos + 1, win).astype(F32)
        dlt = tot / cnt - win_ref[halo:halo + tt, cs]
        o_ref[0, :, cs] = jnp.dot(dlt.astype(BF16), wb_ref[gi], preferred_element_type=F32) * sc_ref[:, cs]
    win_ref[0:halo, :] = win_ref[tt:tt + halo, :]


def _pool_mixer(h, buf, start_pos, w_grp, scale):
    bsz, t, d = h.shape
    ng, group, _ = w_grp.shape
    halo = -(-POOL_BUF // SUBLANES) * SUBLANES
    tt = _tile(t, 256, SUBLANES)
    assert tt >= halo or tt == t
    bufp = jnp.pad(buf, ((0, 0), (halo - POOL_BUF, 0), (0, 0)))
    return pl.pallas_call(
        functools.partial(_pool_body, tt=tt, halo=halo, start_pos=start_pos, group=group),
        grid=(bsz, t // tt),
        in_specs=[pl.BlockSpec((1, tt, d), lambda i, j: (i, j, 0)),
                  pl.BlockSpec((1, halo, d), lambda i, j: (i, 0, 0)),
                  pl.BlockSpec((ng, group, group), lambda i, j: (0, 0, 0)),
                  pl.BlockSpec((1, d), lambda i, j: (0, 0))],
        out_specs=pl.BlockSpec((1, tt, d), lambda i, j: (i, j, 0)),
        out_shape=jax.ShapeDtypeStruct((bsz, t, d), F32),
        scratch_shapes=[pltpu.VMEM((halo + max(tt, halo), d), F32), pltpu.VMEM((ng, group, group), BF16)],
        compiler_params=_params("arbitrary", "arbitrary"),
        name="pool_mixer",
    )(h, bufp, w_grp, scale.reshape(1, d))


def _keep_expert_rows(meta_ref, i, new, out_ref):
    rows = lax.broadcasted_iota(jnp.int32, new.shape, 0)
    mine = (rows >= meta_ref[5, i]) & (rows < meta_ref[6, i])

    @pl.when(meta_ref[7, i] == 1)
    def _():
        out_ref[...] = jnp.where(mine, new, 0.0).astype(out_ref.dtype)

    @pl.when(meta_ref[7, i] == 0)
    def _():
        out_ref[...] = jnp.where(mine, new.astype(out_ref.dtype), out_ref[...])


def _stream_run_weights(meta_ref, cnt_ref, start_copy, stage, sem, cast):
    j, i = pl.program_id(0), pl.program_id(1)
    n_used, n_runs = cnt_ref[0], cnt_ref[1]

    @pl.when((i < n_used) & (meta_ref[1, i] == 1))
    def _():
        run = meta_ref[2, i]
        g = j * n_runs + run
        slot = lax.rem(g, 2)

        @pl.when(g == 0)
        def _():
            start_copy(meta_ref[0, 0], 0, 0)

        pltpu.make_async_copy(stage.at[slot], stage.at[slot], sem.at[slot]).wait()
        last_run = run + 1 == n_runs

        @pl.when(jnp.logical_not(last_run) | (j + 1 < pl.num_programs(0)))
        def _():
            start_copy(meta_ref[3, i], jnp.where(last_run, j + 1, j), 1 - slot)

        cast(slot)


def _moe_up_body(meta_ref, cnt_ref, x_ref, w_hbm, bg_ref, bu_ref, h_ref, stage, wb_ref, sem, *, layer, tf, f):
    def start_copy(e, tile, slot):
        col = pl.multiple_of(tile * tf, tf)
        w_e = w_hbm.at[layer, e]
        pltpu.make_async_copy(w_e.at[:, pl.ds(col, tf)], stage.at[slot, 0],
                              sem.at[slot]).start(priority=WEIGHT_DMA_PRIORITY)
        pltpu.make_async_copy(w_e.at[:, pl.ds(f + col, tf)], stage.at[slot, 1],
                              sem.at[slot]).start(priority=WEIGHT_DMA_PRIORITY)

    def cast(slot):
        wb_ref[0] = stage[slot, 0].astype(BF16)
        wb_ref[1] = stage[slot, 1].astype(BF16)

    _stream_run_weights(meta_ref, cnt_ref, start_copy, stage, sem, cast)
    i = pl.program_id(1)

    @pl.when(i < cnt_ref[0])
    def _():
        x = x_ref[...]
        g = jnp.dot(x, wb_ref[0], preferred_element_type=F32) + bg_ref[...]
        u = jnp.dot(x, wb_ref[1], preferred_element_type=F32) + bu_ref[...]
        g = jnp.minimum(g, SWIGLU_LIMIT)
        u = jnp.clip(u, -SWIGLU_LIMIT, SWIGLU_LIMIT)
        _keep_expert_rows(meta_ref, i, (u + 1.0) * g * jax.nn.sigmoid(SWIGLU_ALPHA * g), h_ref)


def _moe_down_body(meta_ref, cnt_ref, h_ref, w_hbm, b_ref, y_ref, stage, wb_ref, sem, *, layer, tn):
    def start_copy(e, tile, slot):
        col = pl.multiple_of(tile * tn, tn)
        pltpu.make_async_copy(w_hbm.at[layer, e].at[:, pl.ds(col, tn)], stage.at[slot],
                              sem.at[slot]).start(priority=WEIGHT_DMA_PRIORITY)

    def cast(slot):
        wb_ref[...] = stage[slot].astype(BF16)

    _stream_run_weights(meta_ref, cnt_ref, start_copy, stage, sem, cast)
    i = pl.program_id(1)

    @pl.when(i < cnt_ref[0])
    def _():
        y = jnp.dot(h_ref[...], wb_ref[...], preferred_element_type=F32) + b_ref[...]
        _keep_expert_rows(meta_ref, i, y, y_ref)


def _moe_items(counts, n_blocks, bm):
    n_exp = counts.shape[0]
    n_items_max = n_blocks + n_exp - 1
    end = jnp.cumsum(counts)
    start = end - counts
    first_blk = start // bm
    n_e = jnp.where(counts > 0, (end - 1) // bm - first_blk + 1, 0)
    item_end = jnp.cumsum(n_e)
    n_items = item_end[-1]
    w = jnp.arange(n_items_max, dtype=jnp.int32)
    live = w < n_items
    wc = jnp.minimum(w, n_items - 1)
    e = jnp.minimum(jnp.sum((item_end[None, :] <= wc[:, None]).astype(jnp.int32), axis=1), n_exp - 1)
    blk = first_blk[e] + wc - (item_end[e] - n_e[e])
    lo = jnp.where(live, jnp.clip(start[e] - blk * bm, 0, bm), 0)
    hi = jnp.where(live, jnp.clip(end[e] - blk * bm, 0, bm), 0)
    opens = live & ((w == 0) | (e != jnp.roll(e, 1)))
    run = jnp.cumsum(opens.astype(jnp.int32)) - 1
    starts = jnp.where(opens, w, n_items_max)
    later = jnp.concatenate([lax.cummin(starts, reverse=True)[1:], jnp.full((1,), n_items_max, jnp.int32)])
    nxt_e = e[jnp.where(later < n_items_max, later, 0)]
    first_visit = live & ((w == 0) | (blk != jnp.roll(blk, 1)))
    meta = jnp.stack([e, opens, run, nxt_e, blk, lo, hi, first_visit]).astype(jnp.int32)
    cnt = jnp.stack([n_items, jnp.sum(opens.astype(jnp.int32))]).astype(jnp.int32)
    return meta, cnt, n_items_max


def _moe_experts(xs, counts, w_gu, b_gu, w_down, b_down, layer):
    n_rows, d = xs.shape
    _, n_exp, _, f2 = w_gu.shape
    f = f2 // 2
    bm = MOE_ROWS
    meta, cnt, n_items = _moe_items(counts, n_rows // bm, bm)
    tf = _tile(f, MOE_COLS, LANES)
    nf = f // tf
    b_gu3 = b_gu[layer].reshape(n_exp, 1, f2)
    hidden = pl.pallas_call(
        functools.partial(_moe_up_body, layer=layer, tf=tf, f=f),
        grid_spec=pltpu.PrefetchScalarGridSpec(
            num_scalar_prefetch=2,
            grid=(nf, n_items),
            in_specs=[pl.BlockSpec((bm, d), lambda j, i, mt, ct: (mt[4, i], 0)),
                      pl.BlockSpec(memory_space=pl.ANY),
                      pl.BlockSpec((None, 1, tf), lambda j, i, mt, ct: (mt[0, i], 0, j)),
                      pl.BlockSpec((None, 1, tf), lambda j, i, mt, ct: (mt[0, i], 0, j + nf))],
            out_specs=pl.BlockSpec((bm, tf), lambda j, i, mt, ct: (mt[4, i], j)),
            scratch_shapes=[pltpu.VMEM((2, 2, d, tf), F32), pltpu.VMEM((2, d, tf), BF16),
                            pltpu.SemaphoreType.DMA((2,))]),
        out_shape=jax.ShapeDtypeStruct((n_rows, f), BF16),
        compiler_params=_params("arbitrary", "arbitrary"),
        name="moe_up",
    )(meta, cnt, xs, w_gu, b_gu3, b_gu3)
    tn = _tile(d, MOE_COLS, LANES)
    return pl.pallas_call(
        functools.partial(_moe_down_body, layer=layer, tn=tn),
        grid_spec=pltpu.PrefetchScalarGridSpec(
            num_scalar_prefetch=2,
            grid=(d // tn, n_items),
            in_specs=[pl.BlockSpec((bm, f), lambda j, i, mt, ct: (mt[4, i], 0)),
                      pl.BlockSpec(memory_space=pl.ANY),
                      pl.BlockSpec((None, 1, tn), lambda j, i, mt, ct: (mt[0, i], 0, j))],
            out_specs=pl.BlockSpec((bm, tn), lambda j, i, mt, ct: (mt[4, i], j)),
            scratch_shapes=[pltpu.VMEM((2, f, tn), F32), pltpu.VMEM((f, tn), BF16),
                            pltpu.SemaphoreType.DMA((2,))]),
        out_shape=jax.ShapeDtypeStruct((n_rows, d), F32),
        compiler_params=_params("arbitrary", "arbitrary"),
        name="moe_down",
    )(meta, cnt, hidden, w_down, b_down[layer].reshape(n_exp, 1, d))


def _dispatch_body(idx_ref, nxt_ref, x_hbm, o_ref, buf, sem, *, bm):
    step = pl.program_id(0)
    slot = lax.rem(step, 2)

    def start_block(ref, dst_slot):
        def body(r2, carry):
            for q in range(2):
                r = 2 * r2 + q
                pltpu.make_async_copy(x_hbm.at[pl.ds(ref[0, r], 1)], buf.at[dst_slot, pl.ds(r, 1)],
                                      sem.at[dst_slot]).start(priority=q)
            return carry
        lax.fori_loop(0, bm // 2, body, 0, unroll=8)

    @pl.when(step == 0)
    def _():
        start_block(idx_ref, 0)

    @pl.when(step + 1 < pl.num_programs(0))
    def _():
        start_block(nxt_ref, 1 - slot)

    pltpu.make_async_copy(buf.at[slot], buf.at[slot], sem.at[slot]).wait()
    o_ref[...] = buf[slot].astype(o_ref.dtype)


def _dispatch(tokens, row_tok):
    n_rows = row_tok.shape[0]
    d = tokens.shape[1]
    bm = MOE_ROWS
    n_blocks = n_rows // bm
    idx3 = row_tok.astype(jnp.int32).reshape(n_blocks, 1, bm)
    return pl.pallas_call(
        functools.partial(_dispatch_body, bm=bm),
        grid=(n_blocks,),
        in_specs=[pl.BlockSpec((None, 1, bm), lambda i: (i, 0, 0), memory_space=pltpu.SMEM),
                  pl.BlockSpec((None, 1, bm), lambda i: (jnp.minimum(i + 1, n_blocks - 1), 0, 0),
                               memory_space=pltpu.SMEM),
                  pl.BlockSpec(memory_space=pl.ANY)],
        out_specs=pl.BlockSpec((bm, d), lambda i: (i, 0)),
        out_shape=jax.ShapeDtypeStruct((n_rows, d), BF16),
        scratch_shapes=[pltpu.VMEM((2, bm, d), F32), pltpu.SemaphoreType.DMA((2,))],
        compiler_params=_params("arbitrary"),
        name="moe_dispatch",
    )(idx3, idx3, tokens)


def _moe(hf, w_router, b_router, w_gu, b_gu, w_down, b_down, layer):
    n, d = hf.shape
    n_exp = w_router.shape[2]
    bm = MOE_ROWS
    logits = _matmul(hf, w_router, b_router, layer, exact=True)
    top_v, top_e = lax.top_k(logits, TOP_K)
    gates = jax.nn.softmax(top_v, axis=-1)
    n_assign = n * TOP_K
    flat_e = top_e.reshape(n_assign).astype(jnp.int32)
    counts = jnp.sum((flat_e[:, None] == jnp.arange(n_exp, dtype=jnp.int32)[None, :]).astype(jnp.int32), axis=0)
    ids = jnp.arange(n_assign, dtype=jnp.int32)
    _, order = lax.sort((flat_e, ids), num_keys=1, is_stable=True)
    _, pos = lax.sort((order, ids), num_keys=1)
    n_rows = -(-n_assign // bm) * bm
    row_tok = jnp.pad(order // TOP_K, (0, n_rows - n_assign))
    y = _moe_experts(_dispatch(hf, row_tok), counts, w_gu, b_gu, w_down, b_down, layer)
    return y, pos.reshape(n, TOP_K), gates


def _combine_body(pos_ref, nxt_ref, y_hbm, eg_ref, x_ref, gt_ref, g_ref, *rest, tm, final):
    if final:
        o_ref, buf, sem = rest
    else:
        sh_ref, sc_ref, xo_ref, h_ref, buf, sem = rest
    step = pl.program_id(0) * pl.num_programs(1) + pl.program_id(1)
    n_steps = pl.num_programs(0) * pl.num_programs(1)
    slot = lax.rem(step, 2)

    def start_block(idx_ref, dst_slot):
        def body(r, carry):
            for k in range(TOP_K):
                row = idx_ref[0, r * TOP_K + k]
                pltpu.make_async_copy(y_hbm.at[pl.ds(row, 1)], buf.at[dst_slot, k, pl.ds(r, 1)],
                                      sem.at[dst_slot]).start(priority=k % 2)
            return carry
        lax.fori_loop(0, tm, body, 0, unroll=8)

    @pl.when(step == 0)
    def _():
        start_block(pos_ref, 0)

    @pl.when(step + 1 < n_steps)
    def _():
        start_block(nxt_ref, 1 - slot)

    pltpu.make_async_copy(buf.at[slot], buf.at[slot], sem.at[slot]).wait()
    eg = eg_ref[0]
    moe = eg[:, 0:1] * buf[slot, 0]
    for k in range(1, TOP_K):
        moe = moe + eg[:, k:k + 1] * buf[slot, k]
    x = x_ref[0] + gt_ref[0] * moe
    if final:
        o_ref[0] = (x * lax.rsqrt(jnp.mean(x * x, axis=-1, keepdims=True) + RMS_EPS)) * g_ref[...]
    else:
        xo_ref[0] = x
        h_ref[0] = _rms_mod(x, g_ref[...], sh_ref[0], sc_ref[0]).astype(h_ref.dtype)


def _moe_combine(y, pos, expert_gates, x, gate, g, shift=None, scale=None):
    bsz, t, d = x.shape
    final = shift is None
    tm = _tile(t, 128, SUBLANES)
    nt = t // tm
    n_steps = bsz * nt
    pos3 = pos.astype(jnp.int32).reshape(n_steps, 1, tm * TOP_K)
    idx = pl.BlockSpec((None, 1, tm * TOP_K), lambda i, j: (i * nt + j, 0, 0), memory_space=pltpu.SMEM)
    nxt = pl.BlockSpec((None, 1, tm * TOP_K), lambda i, j: (jnp.minimum(i * nt + j + 1, n_steps - 1), 0, 0),
                       memory_space=pltpu.SMEM)
    row = pl.BlockSpec((1, tm, d), lambda i, j: (i, j, 0))
    vec = pl.BlockSpec((1, d), lambda i, j: (0, 0))
    in_specs = [idx, nxt, pl.BlockSpec(memory_space=pl.ANY), pl.BlockSpec((1, tm, TOP_K), lambda i, j: (i, j, 0)),
                row, _mod_spec(gate, tm), vec]
    args = [pos3, pos3, y, expert_gates.reshape(bsz, t, TOP_K), x, gate, g.reshape(1, d)]
    if final:
        out_specs, out_shape = row, jax.ShapeDtypeStruct((bsz, t, d), F32)
    else:
        in_specs += [_mod_spec(shift, tm), _mod_spec(scale, tm)]
        args += [shift, scale]
        out_specs = [row, row]
        out_shape = [jax.ShapeDtypeStruct((bsz, t, d), F32), jax.ShapeDtypeStruct((bsz, t, d), F32)]
    return pl.pallas_call(
        functools.partial(_combine_body, tm=tm, final=final),
        grid=(bsz, nt),
        in_specs=in_specs,
        out_specs=out_specs,
        out_shape=out_shape,
        scratch_shapes=[pltpu.VMEM((2, TOP_K, tm, d), F32), pltpu.SemaphoreType.DMA((2,))],
        compiler_params=_params("arbitrary", "arbitrary"),
        name="moe_combine",
    )(*args)


def _head_sums(x):
    li = lax.broadcasted_iota(jnp.int32, (LANES, LANES), 0) // HEAD_SIZE
    lj = lax.broadcasted_iota(jnp.int32, (LANES, LANES), 1) // HEAD_SIZE
    same_head = (li == lj).astype(F32)
    cols = [jnp.dot(x[:, c * LANES:(c + 1) * LANES], same_head, precision=HIGHEST, preferred_element_type=F32)
            for c in range(x.shape[1] // LANES)]
    return jnp.concatenate(cols, axis=1)


def _rwkv_prep_body(k_ref, al_ref, wl_ref, *rest, vres):
    if vres:
        v_ref, vf_ref, vl_ref, w0, a0, kkw, kaw, v0, lw_o, k_o, a_o, b_o, v_o = rest
    else:
        w0, a0, kkw, kaw, lw_o, k_o, a_o, b_o = rest
    k = k_ref[...]
    a = jax.nn.sigmoid(a0[...] + al_ref[...])
    z = -(w0[...] + wl_ref[...])
    softplus = jnp.maximum(z, 0.0) + jnp.log(1.0 + jnp.exp(-jnp.abs(z)))
    lw_o[...] = -jnp.exp(-softplus - 0.5)
    kk = k * kkw[...]
    kk = kk / jnp.maximum(jnp.sqrt(_head_sums(kk * kk)), 1e-12)
    k_o[...] = k * (1.0 + (a - 1.0) * kaw[...])
    a_o[...] = -kk
    b_o[...] = kk * a
    if vres:
        v = v_ref[...]
        v_o[...] = v + (vf_ref[...] - v) * jax.nn.sigmoid(v0[...] + vl_ref[...])


def _rwkv_prep(k, a_lora, w_lora, w0, a0, kk_w, ka_w, vres=None):
    n, d = k.shape
    tm = _tile(n, 128, SUBLANES)
    row = pl.BlockSpec((tm, d), lambda i: (i, 0))
    vec = pl.BlockSpec((1, d), lambda i: (0, 0))
    rows = [k, a_lora, w_lora]
    vecs = [w0, a0, kk_w, ka_w]
    if vres is not None:
        v, v_first, v_lora, v0 = vres
        rows += [v, v_first, v_lora]
        vecs.append(v0)
    n_out = 4 if vres is None else 5
    return pl.pallas_call(
        functools.partial(_rwkv_prep_body, vres=vres is not None),
        grid=(n // tm,),
        in_specs=[row] * len(rows) + [vec] * len(vecs),
        out_specs=[row] * n_out,
        out_shape=[jax.ShapeDtypeStruct((n, d), F32)] * n_out,
        compiler_params=_params("arbitrary"),
        name="rwkv_prep",
    )(*rows, *(z.reshape(1, d) for z in vecs))


def _rwkv_post_body(y_ref, r_ref, k_ref, v_ref, g_ref, lnw, lnb, rk, o_ref):
    y = y_ref[...]
    yc = y - _head_sums(y) * (1.0 / HEAD_SIZE)
    var = _head_sums(yc * yc) * (1.0 / HEAD_SIZE)
    yn = yc * lax.rsqrt(var + LN_X_EPS) * lnw[...] + lnb[...]
    bonus = _head_sums(r_ref[...] * k_ref[...] * rk[...]) * v_ref[...]
    o_ref[...] = ((yn + bonus) * g_ref[...]).astype(o_ref.dtype)


def _rwkv_post(y, r, k, v, g, lnx_w, lnx_b, r_k):
    n, d = y.shape
    tm = _tile(n, 128, SUBLANES)
    row = pl.BlockSpec((tm, d), lambda i: (i, 0))
    vec = pl.BlockSpec((1, d), lambda i: (0, 0))
    return pl.pallas_call(
        _rwkv_post_body,
        grid=(n // tm,),
        in_specs=[row] * 5 + [vec] * 3,
        out_specs=row,
        out_shape=jax.ShapeDtypeStruct((n, d), BF16),
        compiler_params=_params("arbitrary"),
        name="rwkv_post",
    )(y, r, k, v, g, lnx_w.reshape(1, d), lnx_b.reshape(1, d), r_k.reshape(1, d))


def _rwkv_mix_body(h_ref, sp_ref, mix_ref, *rest):
    outs, carry = rest[:-1], rest[-1]

    @pl.when(pl.program_id(1) == 0)
    def _():
        carry[...] = sp_ref[0]

    h = h_ref[0]
    first = lax.broadcasted_iota(jnp.int32, h.shape, 0) == 0
    xx = jnp.where(first, carry[...], pltpu.roll(h, 1, 0)) - h
    for m, o_ref in enumerate(outs):
        o_ref[0] = (h + xx * mix_ref[m:m + 1, :]).astype(o_ref.dtype)
    carry[...] = h[h.shape[0] - 1:, :]


def _rwkv_mix(h, shift_prev, mix):
    bsz, t, d = h.shape
    n_mix = mix.shape[0]
    tt = _tile(t, 256, SUBLANES)
    row = pl.BlockSpec((1, tt, d), lambda b, i: (b, i, 0))
    outs = pl.pallas_call(
        _rwkv_mix_body,
        grid=(bsz, t // tt),
        in_specs=[row, pl.BlockSpec((1, 1, d), lambda b, i: (b, 0, 0)), pl.BlockSpec((n_mix, d), lambda b, i: (0, 0))],
        out_specs=[row] * n_mix,
        out_shape=[jax.ShapeDtypeStruct((bsz, t, d), BF16)] * n_mix,
        scratch_shapes=[pltpu.VMEM((1, d), F32)],
        compiler_params=_params("arbitrary", "arbitrary"),
        name="rwkv_mix",
    )(h, shift_prev.reshape(bsz, 1, d), mix)
    return [o.reshape(bsz * t, d) for o in outs]


def _rwkv7(h, shift_prev, s0, v_first, p, j):
    bsz, t, d = h.shape
    n = bsz * t
    xr, xw, xk, xv, xa, xg = _rwkv_mix(h, shift_prev, p["rwkv_mix"][j])
    r = _matmul(xr, p["rwkv_wr"], layer=j)
    k = _matmul(xk, p["rwkv_wk"], layer=j)
    v = _matmul(xv, p["rwkv_wv"], layer=j)
    w_lora = _matmul(_matmul(xw, p["rwkv_w1"], layer=j, act="tanh", out_dtype=BF16), p["rwkv_w2"], layer=j)
    a_lora = _matmul(_matmul(xa, p["rwkv_a1"], layer=j, out_dtype=BF16), p["rwkv_a2"], layer=j)
    g = _matmul(_matmul(xg, p["rwkv_g1"], layer=j, act="sigmoid", out_dtype=BF16), p["rwkv_g2"], layer=j)
    vecs = (p["rwkv_w0"][j], p["rwkv_a0"][j], p["rwkv_kk"][j], p["rwkv_ka"][j])
    if j == 0:
        v_first = v
        log_decay, k, a_in, b_in = _rwkv_prep(k, a_lora, w_lora, *vecs)
    else:
        v_lora = _matmul(_matmul(xv, p["rwkv_v1"], layer=j - 1, out_dtype=BF16), p["rwkv_v2"], layer=j - 1)
        log_decay, k, a_in, b_in, v = _rwkv_prep(k, a_lora, w_lora, *vecs,
                                                 vres=(v, v_first, v_lora, p["rwkv_v0"][j - 1]))
    seq = lambda z: z.reshape(bsz, t, d)
    y, s_t = _wkv(seq(r), seq(log_decay), seq(k), seq(v), seq(a_in), seq(b_in), s0, j)
    gated = _rwkv_post(y.reshape(n, d), r, k, v, g, p["rwkv_lnx_w"][j], p["rwkv_lnx_b"][j], p["rwkv_rk"][j])
    out = _matmul(gated, p["rwkv_wo"], layer=j)
    return out.reshape(bsz, t, d), h[:, -1], s_t, v_first


def _conformer_conv(h, buf, p, j):
    bsz, t, d = h.shape
    width = p["conv_w_dw"].shape[1]
    u = _glu_matmul(h.reshape(bsz * t, d), p["conv_w_in"], p["conv_b_in"], j).reshape(bsz, t, -1)
    z = _dwconv_ln_silu(u, buf, p["conv_w_dw"][j], p["conv_b_dw"][j], p["conv_ln_w"][j], p["conv_ln_b"][j])
    out = _matmul(z.reshape(bsz * t, -1), p["conv_w_out"], p["conv_b_out"], j).reshape(bsz, t, d)
    new_buf = jnp.concatenate([buf, u], axis=1)[:, -(width - 1):]
    return out, new_buf


def _pool(h, buf, start_pos, p, j):
    out = _pool_mixer(h, buf, start_pos, p["pool_w"][j], p["pool_scale"][j])
    return out, jnp.concatenate([buf, h], axis=1)[:, -POOL_BUF:]


def kernel(x_prompt, x_sample, c_prompt, c_sample, state_wkv, state_shift, state_conv, state_pool, ada_w, ada_b, norm_mix, norm_ffn, final_norm, rwkv_mix, rwkv_wr, rwkv_wk, rwkv_wv, rwkv_wo, rwkv_w0, rwkv_w1, rwkv_w2, rwkv_a0, rwkv_a1, rwkv_a2, rwkv_v0, rwkv_v1, rwkv_v2, rwkv_g1, rwkv_g2, rwkv_kk, rwkv_ka, rwkv_rk, rwkv_lnx_w, rwkv_lnx_b, conv_w_in, conv_b_in, conv_w_dw, conv_b_dw, conv_ln_w, conv_ln_b, conv_w_out, conv_b_out, pool_w, pool_scale, moe_w_router, moe_b_router, moe_w_gu, moe_b_gu, moe_w_down, moe_b_down):
    p = dict(rwkv_mix=rwkv_mix, rwkv_wr=rwkv_wr, rwkv_wk=rwkv_wk, rwkv_wv=rwkv_wv, rwkv_wo=rwkv_wo,
             rwkv_w0=rwkv_w0, rwkv_w1=rwkv_w1, rwkv_w2=rwkv_w2, rwkv_a0=rwkv_a0, rwkv_a1=rwkv_a1,
             rwkv_a2=rwkv_a2, rwkv_v0=rwkv_v0, rwkv_v1=rwkv_v1, rwkv_v2=rwkv_v2, rwkv_g1=rwkv_g1,
             rwkv_g2=rwkv_g2, rwkv_kk=rwkv_kk, rwkv_ka=rwkv_ka, rwkv_rk=rwkv_rk, rwkv_lnx_w=rwkv_lnx_w,
             rwkv_lnx_b=rwkv_lnx_b, conv_w_in=conv_w_in, conv_b_in=conv_b_in, conv_w_dw=conv_w_dw,
             conv_b_dw=conv_b_dw, conv_ln_w=conv_ln_w, conv_ln_b=conv_ln_b, conv_w_out=conv_w_out,
             conv_b_out=conv_b_out, pool_w=pool_w, pool_scale=pool_scale)
    depth = ada_w.shape[0]
    bp, tp, d = x_prompt.shape
    bs, ts, _ = x_sample.shape
    heads = d // HEAD_SIZE
    n_p, n_s = bp * tp, bs * ts
    width = conv_w_dw.shape[1]

    c_act = jax.nn.silu(jnp.concatenate([c_prompt, c_sample], axis=0))
    mods = []
    for i in range(depth):
        m = _matmul(c_act, ada_w, ada_b, i)
        six = jnp.split(m, 6, axis=-1)
        mods.append(([z[:bp, None, :] for z in six],
                     [jnp.repeat(z[bp:], ts, axis=0)[None] for z in six]))

    groups = [
        dict(x=x_prompt, b=bp, t=tp, start=0,
             wkv=jnp.zeros((state_wkv.shape[0], bp, heads, HEAD_SIZE, HEAD_SIZE), F32),
             shift=jnp.zeros((state_shift.shape[0], bp, d), F32),
             conv=jnp.zeros((state_conv.shape[0], bp, width - 1, conv_w_dw.shape[2]), F32),
             pool=jnp.zeros((state_pool.shape[0], bp, POOL_BUF, d), F32)),
        dict(x=x_sample.reshape(1, n_s, d), b=bs, t=ts, start=PAST_LEN,
             wkv=state_wkv, shift=state_shift, conv=state_conv, pool=state_pool),
    ]
    for gi, grp in enumerate(groups):
        grp.update(wkv_out=[], shift_out=[], conv_out=[], pool_out=[], v_first=None)
        sh1, sc1 = mods[0][gi][0], mods[0][gi][1]
        grp["h"] = _normmod(grp["x"], norm_mix[0], sh1, sc1, F32)

    for i in range(depth):
        kind = i % 3
        tokens = jnp.zeros((n_p + n_s, d), F32)
        for gi, grp in enumerate(groups):
            sh1, sc1, gt1, sh2, sc2, gt2 = mods[i][gi]
            h = grp["h"].reshape(grp["b"], grp["t"], d)
            if kind == 0:
                j = len(grp["wkv_out"])
                out, last, s_t, grp["v_first"] = _rwkv7(h, grp["shift"][j], grp["wkv"], grp["v_first"], p, j)
                grp["wkv_out"].append(s_t)
                grp["shift_out"].append(last)
            elif kind == 1:
                j = len(grp["conv_out"])
                out, buf = _conformer_conv(h, grp["conv"][j], p, j)
                grp["conv_out"].append(buf)
            else:
                j = len(grp["pool_out"])
                out, buf = _pool(h, grp["pool"][j], grp["start"], p, j)
                grp["pool_out"].append(buf)
            grp["x"], tokens = _resnormmod(grp["x"], out.reshape(grp["x"].shape), gt1, norm_ffn[i], sh2, sc2,
                                           tokens, gi * n_p)
        y, pos, egates = _moe(tokens, moe_w_router, moe_b_router, moe_w_gu, moe_b_gu, moe_w_down, moe_b_down, i)
        for gi, grp in enumerate(groups):
            gt2 = mods[i][gi][5]
            part = slice(0, n_p) if gi == 0 else slice(n_p, None)
            if i + 1 < depth:
                sh1, sc1 = mods[i + 1][gi][0], mods[i + 1][gi][1]
                grp["x"], grp["h"] = _moe_combine(y, pos[part], egates[part], grp["x"], gt2, norm_mix[i + 1],
                                                  sh1, sc1)
            else:
                grp["x"] = _moe_combine(y, pos[part], egates[part], grp["x"], gt2, final_norm)

    gp, gs = groups
    return (gp["x"], gs["x"].reshape(bs, ts, d),
            jnp.stack(gp["wkv_out"]), jnp.stack(gp["shift_out"]), jnp.stack(gp["conv_out"]), jnp.stack(gp["pool_out"]),
            jnp.stack(gs["wkv_out"]), jnp.stack(gs["shift_out"]), jnp.stack(gs["conv_out"]), jnp.stack(gs["pool_out"]))
```
